```python
import jax, jax.numpy as jnp
from jax import lax
import numpy as np

D_MODEL = 1024
BATCH = 4
SEQ = 8192
DEPTH = 1

CONV_WIDTH = 1024
CONV_SIZE = 31
HEAD_DIM = 64
HEADS_PER_GROUP = 8
DILATION_GROUPS = ((128, 1), (512, 4), (2048, 16))
N_GROUPS = len(DILATION_GROUPS)
N_ATT_HEADS = N_GROUPS * HEADS_PER_GROUP
ATT_QKV = N_ATT_HEADS * HEAD_DIM
ATT_OUT = HEADS_PER_GROUP * HEAD_DIM
ROT_DIM = HEAD_DIM // 4
ROPE_THETA = 500000.0
BLOCK = 128
MAX_POS_OFFSET = 4096
EPS = 1e-6
NEG_INF = -1e30

IN_SPLITS = (CONV_WIDTH, CONV_WIDTH, CONV_WIDTH,
             ATT_QKV, ATT_QKV, ATT_QKV, ATT_OUT,
             D_MODEL, D_MODEL)
IN_COLS = sum(IN_SPLITS)

kernel_name = "hybrid_conformer_conv_dilated_attention_gated_merge"


def _rmsnorm(x, g):
    xf = x.astype(jnp.float32)
    y = xf * lax.rsqrt(jnp.mean(xf * xf, axis=-1, keepdims=True) + EPS)
    return (y * g.astype(jnp.float32)).astype(x.dtype)


def _layernorm(x, g, b):
    xf = x.astype(jnp.float32)
    mu = jnp.mean(xf, axis=-1, keepdims=True)
    var = jnp.mean(jnp.square(xf - mu), axis=-1, keepdims=True)
    y = (xf - mu) * lax.rsqrt(var + EPS)
    return (y * g.astype(jnp.float32) + b.astype(jnp.float32)).astype(x.dtype)


def _partial_rope(t, positions):
    half = ROT_DIM // 2
    inv_freq = ROPE_THETA ** (-(jnp.arange(half, dtype=jnp.float32) * 2.0 / ROT_DIM))
    ang = positions.astype(jnp.float32)[..., None] * inv_freq
    cos = jnp.cos(ang)[:, :, None, :]
    sin = jnp.sin(ang)[:, :, None, :]
    tf = t.astype(jnp.float32)
    t1, t2 = tf[..., :half], tf[..., half:ROT_DIM]
    out = jnp.concatenate([t1 * cos - t2 * sin, t2 * cos + t1 * sin, tf[..., ROT_DIM:]], axis=-1)
    return out.astype(t.dtype)


def _dilated_window_group(q, k, v, window, dilation):
    b, s, h, e = q.shape
    L = s // dilation
    w_sub = window // dilation
    nb = -(-L // BLOCK)
    lp = nb * BLOCK

    def to_sub(t):
        return t.reshape(b, L, dilation, h, e).transpose(0, 2, 3, 1, 4)

    qs, ks, vs = to_sub(q), to_sub(k), to_sub(v)
    qs = jnp.pad(qs, ((0, 0), (0, 0), (0, 0), (0, lp - L), (0, 0)))
    ks = jnp.pad(ks, ((0, 0), (0, 0), (0, 0), (BLOCK, lp - L), (0, 0)))
    vs = jnp.pad(vs, ((0, 0), (0, 0), (0, 0), (BLOCK, lp - L), (0, 0)))
    qb = qs.reshape(b, dilation, h, nb, BLOCK, e)

    def band(t):
        prev = t[:, :, :, :lp].reshape(b, dilation, h, nb, BLOCK, e)
        cur = t[:, :, :, BLOCK:].reshape(b, dilation, h, nb, BLOCK, e)
        return jnp.concatenate([prev, cur], axis=-2)

    kb, vb = band(ks), band(vs)
    scores = jnp.einsum('bdhnqe,bdhnke->bdhnqk', qb.astype(jnp.float32),
                        kb.astype(jnp.float32)) * (e ** -0.5)
    qi = jnp.arange(BLOCK)[:, None]
    kj = jnp.arange(2 * BLOCK)[None, :]
    dist = qi + BLOCK - kj
    key_idx = jnp.arange(nb)[:, None, None] * BLOCK - BLOCK + kj[None]
    mask = (dist >= 0) & (dist <= w_sub) & (key_idx >= 0)
    scores = jnp.where(mask, scores, NEG_INF)
    m = jnp.max(scores, axis=-1)
    p = jnp.exp(scores - m[..., None])
    den = jnp.sum(p, axis=-1)
    o = jnp.einsum('bdhnqk,bdhnke->bdhnqe', p, vb.astype(jnp.float32)) / den[..., None]

    def from_sub(t):
        tail = t.shape[5:]
        t = t.reshape((b, dilation, h, lp) + tail)[:, :, :, :L]
        t = jnp.moveaxis(t, 3, 1)
        return t.reshape((b, s, h) + tail)

    return from_sub(o), from_sub(m), from_sub(den)


def setup_inputs(seed: int = 0) -> dict:
    key = jax.random.key(seed)
    ks = jax.random.split(key, 16)
    f32 = jnp.float32
    x = jax.random.normal(ks[0], (BATCH, SEQ, D_MODEL), f32)
    c = jax.random.normal(ks[1], (BATCH, D_MODEL), f32)
    positions = (jnp.arange(SEQ, dtype=jnp.int32)[None, :]
                 + jax.random.randint(ks[2], (BATCH, 1), 0, MAX_POS_OFFSET, dtype=jnp.int32))
    norm_g = 1.0 + 0.05 * jax.random.normal(ks[3], (DEPTH, D_MODEL), f32)
    w_ada = 0.5 * D_MODEL ** -0.5 * jax.random.normal(ks[4], (DEPTH, D_MODEL, 3 * D_MODEL), f32)
    b_ada = 0.02 * jax.random.normal(ks[5], (DEPTH, 3 * D_MODEL), f32)
    w_in = D_MODEL ** -0.5 * jax.random.normal(ks[6], (DEPTH, D_MODEL, IN_COLS), f32)
    conv_w = CONV_SIZE ** -0.5 * jax.random.normal(ks[7], (DEPTH, CONV_SIZE, CONV_WIDTH), f32)
    conv_b = 0.02 * jax.random.normal(ks[8], (DEPTH, CONV_WIDTH), f32)
    conv_ln_g = 1.0 + 0.05 * jax.random.normal(ks[9], (DEPTH, CONV_WIDTH), f32)
    conv_ln_b = 0.02 * jax.random.normal(ks[10], (DEPTH, CONV_WIDTH), f32)
    w_conv_out = CONV_WIDTH ** -0.5 * jax.random.normal(ks[11], (DEPTH, CONV_WIDTH, D_MODEL), f32)
    w_att_out = ATT_OUT ** -0.5 * jax.random.normal(ks[12], (DEPTH, ATT_OUT, D_MODEL), f32)
    w_o = D_MODEL ** -0.5 * jax.random.normal(ks[13], (DEPTH, D_MODEL, D_MODEL), f32)
    final_g = 1.0 + 0.05 * jax.random.normal(ks[14], (D_MODEL,), f32)
    return {"x": x, "c": c, "positions": positions, "norm_g": norm_g,
            "w_ada": w_ada, "b_ada": b_ada, "w_in": w_in, "conv_w": conv_w,
            "conv_b": conv_b, "conv_ln_g": conv_ln_g, "conv_ln_b": conv_ln_b,
            "w_conv_out": w_conv_out, "w_att_out": w_att_out, "w_o": w_o,
            "final_g": final_g}


def reference(x, c, positions, norm_g, w_ada, b_ada, w_in, conv_w, conv_b, conv_ln_g,
              conv_ln_b, w_conv_out, w_att_out, w_o, final_g):
    b, s, _ = x.shape
    split_idx = np.cumsum(IN_SPLITS)[:-1].tolist()
    for layer in range(DEPTH):
        mod = c @ w_ada[layer] + b_ada[layer]
        shift, scale, gate = [t[:, None, :] for t in jnp.split(mod, 3, axis=-1)]
        h = _rmsnorm(x, norm_g[layer]) * (1.0 + scale) + shift

        proj = h @ w_in[layer]
        (glu_a, glu_b, z_conv, q, k, v, z_att, g_conv, g_att) = jnp.split(proj, split_idx, axis=-1)

        u = glu_a * jax.nn.sigmoid(glu_b)
        u = lax.conv_general_dilated(
            u, conv_w[layer][:, None, :].astype(u.dtype), window_strides=(1,),
            padding=[(CONV_SIZE - 1, 0)], dimension_numbers=('NWC', 'WIO', 'NWC'),
            feature_group_count=CONV_WIDTH) + conv_b[layer]
        u = jax.nn.silu(_layernorm(u, conv_ln_g[layer], conv_ln_b[layer]))
        y_conv = (u * jax.nn.silu(z_conv)) @ w_conv_out[layer]

        q = _partial_rope(q.reshape(b, s, N_ATT_HEADS, HEAD_DIM), positions)
        k = _partial_rope(k.reshape(b, s, N_ATT_HEADS, HEAD_DIM), positions)
        v = v.reshape(b, s, N_ATT_HEADS, HEAD_DIM)
        outs, maxes, dens = [], [], []
        for gi, (window, dilation) in enumerate(DILATION_GROUPS):
            sl = slice(gi * HEADS_PER_GROUP, (gi + 1) * HEADS_PER_GROUP)
            o_g, m_g, d_g = _dilated_window_group(q[:, :, sl], k[:, :, sl], v[:, :, sl],
                                                  window, dilation)
            outs.append(o_g); maxes.append(m_g); dens.append(d_g)
        m_all = jnp.maximum(jnp.maximum(maxes[0], maxes[1]), maxes[2])
        wts = [d_g * jnp.exp(m_g - m_all) for m_g, d_g in zip(maxes, dens)]
        w_sum = wts[0] + wts[1] + wts[2]
        att = (wts[0][..., None] * outs[0] + wts[1][..., None] * outs[1]
               + wts[2][..., None] * outs[2]) / w_sum[..., None]
        att = att.reshape(b, s, ATT_OUT).astype(x.dtype)
        y_att = (att * jax.nn.silu(z_att)) @ w_att_out[layer]

        merged = jax.nn.sigmoid(g_conv) * y_conv + jax.nn.sigmoid(g_att) * y_att
        x = x + gate * (merged @ w_o[layer])
    return _rmsnorm(x, final_g)
```

```python
import functools

import jax
import jax.numpy as jnp
from jax import lax
from jax.experimental import pallas as pl
from jax.experimental.pallas import tpu as pltpu

HEAD_DIM = 64
HEADS_PER_GROUP = 8
DILATION_GROUPS = ((128, 1), (512, 4), (2048, 16))
ROT_DIM = HEAD_DIM // 4
ROPE_THETA = 500000.0
CONV_SIZE = 31
EPS = 1e-6
NEG_INF = -1e30

LANES = 128
SUBLANES = 8
VMEM_LIMIT_BYTES = 56 * 1024 * 1024

GROUP_COLS = HEADS_PER_GROUP * HEAD_DIM
HEADS_PER_VREG = LANES // HEAD_DIM
PAIRS = GROUP_COLS // LANES
QBLK = 128
CONV_HALO = 32
CONV_ROWS = 64

F32 = jnp.float32
BF16 = jnp.bfloat16


def _dot(a, b):
    return jnp.dot(a, b, preferred_element_type=F32)


def _dot_nt(a, b):
    return lax.dot_general(a, b, (((1,), (1,)), ((), ())), preferred_element_type=F32)


def _sigmoid(x):
    return 1.0 / (1.0 + jnp.exp(-x))


def _silu(x):
    return x * _sigmoid(x)


def _mod_kernel(c_ref, w_ref, b_ref, o_ref):
    o_ref[...] = jnp.dot(c_ref[...], w_ref[...], preferred_element_type=F32,
                         precision=lax.Precision.HIGHEST) + b_ref[...]


def _modulation(c, w_ada, b_ada):
    b, d = c.shape
    n = w_ada.shape[1]
    return pl.pallas_call(
        _mod_kernel,
        grid=(n // d,),
        in_specs=[pl.BlockSpec((b, d), lambda j: (0, 0)),
                  pl.BlockSpec((d, d), lambda j: (0, j)),
                  pl.BlockSpec((1, d), lambda j: (0, j))],
        out_specs=pl.BlockSpec((b, d), lambda j: (0, j)),
        out_shape=jax.ShapeDtypeStruct((b, n), F32),
        compiler_params=pltpu.CompilerParams(dimension_semantics=("arbitrary",),
                                             vmem_limit_bytes=VMEM_LIMIT_BYTES),
        name="mod",
    )(c, w_ada, b_ada.reshape(1, n))


def _h_kernel(x_ref, g_ref, scale_ref, shift_ref, h_ref):
    x = x_ref[0]
    y = x * lax.rsqrt(jnp.mean(x * x, axis=-1, keepdims=True) + EPS)
    h_ref[0] = ((y * g_ref[...]) * (1.0 + scale_ref[0]) + shift_ref[0]).astype(BF16)


def _modulated_norm(x, g, scale, shift, ts):
    b, s, d = x.shape
    vec = pl.BlockSpec((1, 1, d), lambda i, j: (i, 0, 0))
    return pl.pallas_call(
        _h_kernel,
        grid=(b, s // ts),
        in_specs=[pl.BlockSpec((1, ts, d), lambda i, j: (i, j, 0)),
                  pl.BlockSpec((1, d), lambda i, j: (0, 0)), vec, vec],
        out_specs=pl.BlockSpec((1, ts, d), lambda i, j: (i, j, 0)),
        out_shape=jax.ShapeDtypeStruct((b, s, d), BF16),
        compiler_params=pltpu.CompilerParams(dimension_semantics=("arbitrary", "arbitrary"),
                                             vmem_limit_bytes=VMEM_LIMIT_BYTES),
        name="hnorm",
    )(x, g.reshape(1, d), scale.reshape(b, 1, d), shift.reshape(b, 1, d))


def _conv_kernel(h_ref, wa_ref, wb_ref, wz_ref, wg_ref, cw_ref, cb_ref, lg_ref, lb_ref, wo_ref,
                 out_ref, ubuf, cbuf):
    tm = h_ref.shape[1]
    ncb = ubuf.shape[0]
    h = h_ref[0]

    @pl.when(pl.program_id(1) == 0)
    def _():
        ubuf[:, 0:CONV_HALO, :] = jnp.zeros((ncb, CONV_HALO, LANES), F32)

    @pl.when(pl.program_id(1) > 0)
    def _():
        ubuf[:, 0:CONV_HALO, :] = ubuf[:, tm:tm + CONV_HALO, :]

    u = _dot(h, wa_ref[...]) * _sigmoid(_dot(h, wb_ref[...]))
    for cb in range(ncb):
        ubuf[cb, CONV_HALO:CONV_HALO + tm, :] = u[:, cb * LANES:(cb + 1) * LANES]

    first_tap = CONV_HALO - (CONV_SIZE - 1)

    def row_block(rb, carry):
        r0 = pl.multiple_of(rb * CONV_ROWS, CONV_ROWS)
        for cb in range(ncb):
            cols = slice(cb * LANES, (cb + 1) * LANES)
            acc = jnp.broadcast_to(cb_ref[:, cols], (CONV_ROWS, LANES))
            for t in range(CONV_SIZE):
                acc = acc + ubuf[cb, pl.ds(r0 + first_tap + t, CONV_ROWS), :] * cw_ref[t:t + 1, cols]
            cbuf[pl.ds(r0, CONV_ROWS), cols] = acc
        return carry

    lax.fori_loop(0, tm // CONV_ROWS, row_block, 0)

    c = cbuf[...]
    mu = jnp.mean(c, axis=-1, keepdims=True)
    dlt = c - mu
    var = jnp.mean(dlt * dlt, axis=-1, keepdims=True)
    y = _silu(dlt * lax.rsqrt(var + EPS) * lg_ref[...] + lb_ref[...])
    y = y * _silu(_dot(h, wz_ref[...]))
    yc = _dot(y.astype(BF16), wo_ref[...])
    out_ref[0] = (_sigmoid(_dot(h, wg_ref[...])) * yc).astype(BF16)


def _conv_branch(h, w_in, conv_w, conv_b, ln_g, ln_b, w_out, tm):
    b, s, d = h.shape
    cw = conv_w.shape[1]
    wcol = lambda blk: pl.BlockSpec((d, cw), lambda i, j: (0, blk))
    row = pl.BlockSpec((1, cw), lambda i, j: (0, 0))
    return pl.pallas_call(
        _conv_kernel,
        grid=(b, s // tm),
        in_specs=[pl.BlockSpec((1, tm, d), lambda i, j: (i, j, 0)),
                  wcol(0), wcol(1), wcol(2), wcol(8),
                  pl.BlockSpec((CONV_SIZE, cw), lambda i, j: (0, 0)), row, row, row,
                  pl.BlockSpec((cw, d), lambda i, j: (0, 0))],
        out_specs=pl.BlockSpec((1, tm, d), lambda i, j: (i, j, 0)),
        out_shape=jax.ShapeDtypeStruct((b, s, d), BF16),
        scratch_shapes=[pltpu.VMEM((cw // LANES, CONV_HALO + tm, LANES), F32),
                        pltpu.VMEM((tm, cw), F32)],
        compiler_params=pltpu.CompilerParams(dimension_semantics=("arbitrary", "arbitrary"),
                                             vmem_limit_bytes=VMEM_LIMIT_BYTES),
        name="conv_branch",
    )(h, w_in, w_in, w_in, w_in, conv_w, conv_b.reshape(1, cw), ln_g.reshape(1, cw),
      ln_b.reshape(1, cw), w_out)


def _att_kernel(h_ref, pos_ref, invf_ref, sgn_ref, wq_ref, wk_ref, wv_ref, o_ref, lse_ref,
                qbuf, kbuf, vbuf):
    tq = h_ref.shape[1]
    n = pl.program_id(2)
    h = h_ref[0]

    @pl.when(n == 0)
    def _():
        kbuf[0:QBLK, :] = jnp.zeros((QBLK, GROUP_COLS), BF16)
        vbuf[0:QBLK, :] = jnp.zeros((QBLK, GROUP_COLS), BF16)

    @pl.when(n > 0)
    def _():
        kbuf[0:QBLK, :] = kbuf[tq:tq + QBLK, :]
        vbuf[0:QBLK, :] = vbuf[tq:tq + QBLK, :]

    ang = pos_ref[0, 0] * invf_ref[...]
    cs = jnp.cos(ang)
    sn = jnp.sin(ang) * sgn_ref[...]
    lane_t = lax.broadcasted_iota(jnp.int32, (tq, LANES), 1)
    low_half = (lane_t & (HEAD_DIM - 1)) < (ROT_DIM // 2)
    head0_t = lane_t < HEAD_DIM

    def rope(t):
        partner = jnp.where(low_half, pltpu.roll(t, LANES - ROT_DIM // 2, 1),
                            pltpu.roll(t, ROT_DIM // 2, 1))
        return t * cs + partner * sn

    q = _dot(h, wq_ref[...])
    k = _dot(h, wk_ref[...])
    for p in range(PAIRS):
        cols = slice(p * LANES, (p + 1) * LANES)
        qr = rope(q[:, cols]) * (HEAD_DIM ** -0.5)
        qbuf[0, :, cols] = jnp.where(head0_t, qr, 0.0).astype(BF16)
        qbuf[1, :, cols] = jnp.where(head0_t, 0.0, qr).astype(BF16)
        kbuf[QBLK:QBLK + tq, cols] = rope(k[:, cols]).astype(BF16)
    vbuf[QBLK:QBLK + tq, :] = _dot(h, wv_ref[...]).astype(BF16)

    row = lax.broadcasted_iota(jnp.int32, (QBLK, 2 * QBLK), 0)
    col = lax.broadcasted_iota(jnp.int32, (QBLK, 2 * QBLK), 1)
    band = ((col < QBLK) & (col >= row)) | ((col >= QBLK) & (col - QBLK <= row))
    band_first = band & ((col >= QBLK) | (n > 0))
    lane = lax.broadcasted_iota(jnp.int32, (QBLK, LANES), 1)

    for qb in range(tq // QBLK):
        rows = slice(qb * QBLK, (qb + 1) * QBLK)
        win = slice(qb * QBLK, (qb + 2) * QBLK)
        valid = band_first if qb == 0 else band
        stats = jnp.zeros((QBLK, LANES), F32)
        for p in range(PAIRS):
            cols = slice(p * LANES, (p + 1) * LANES)
            kw = kbuf[win, cols]
            vw = vbuf[win, cols]
            outs = []
            for hh in range(HEADS_PER_VREG):
                s = jnp.where(valid, _dot_nt(qbuf[hh, rows, cols], kw), NEG_INF)
                m = jnp.max(s, axis=-1, keepdims=True)
                e = jnp.exp(s - m)
                den = jnp.sum(e, axis=-1, keepdims=True)
                outs.append(_dot(e.astype(BF16), vw) * (1.0 / den))
                stats = jnp.where(lane == p * HEADS_PER_VREG + hh, m + jnp.log(den), stats)
            o_ref[0, rows, cols] = jnp.where(lane < HEAD_DIM, outs[0], outs[1]).astype(BF16)
        lse_ref[0, rows, :] = stats


def _attention_group(h, pos_f, w_in, gi, dilation, tq):
    b, s, d = h.shape
    sub_len = s // dilation
    hv = h.reshape(b, sub_len, dilation * d)
    pos = pos_f.reshape(b, sub_len, dilation).transpose(0, 2, 1)[..., None]
    half = ROT_DIM // 2
    inv_freq = ROPE_THETA ** (-(jnp.arange(half, dtype=F32) * 2.0 / ROT_DIM))
    lane = jnp.arange(LANES) % HEAD_DIM
    invf = jnp.where(lane < ROT_DIM, inv_freq[lane % half], 0.0).reshape(1, LANES)
    sgn = jnp.where(lane < half, -1.0, 1.0).astype(F32).reshape(1, LANES)
    col0 = (d * 3) // GROUP_COLS
    n_grp = len(DILATION_GROUPS)
    wspec = lambda blk: pl.BlockSpec((d, GROUP_COLS), lambda i, r, n: (0, blk))
    const = pl.BlockSpec((1, LANES), lambda i, r, n: (0, 0))
    o, lse = pl.pallas_call(
        _att_kernel,
        grid=(b, dilation, sub_len // tq),
        in_specs=[pl.BlockSpec((1, tq, d), lambda i, r, n: (i, n, r)),
                  pl.BlockSpec((1, 1, tq, 1), lambda i, r, n: (i, r, n, 0)),
                  const, const,
                  wspec(col0 + gi), wspec(col0 + n_grp + gi), wspec(col0 + 2 * n_grp + gi)],
        out_specs=[pl.BlockSpec((1, tq, GROUP_COLS), lambda i, r, n: (i, n, r)),
                   pl.BlockSpec((1, tq, LANES), lambda i, r, n: (i, n, r))],
        out_shape=[jax.ShapeDtypeStruct((b, sub_len, dilation * GROUP_COLS), BF16),
                   jax.ShapeDtypeStruct((b, sub_len, dilation * LANES), F32)],
        scratch_shapes=[pltpu.VMEM((HEADS_PER_VREG, tq, GROUP_COLS), BF16),
                        pltpu.VMEM((QBLK + tq, GROUP_COLS), BF16),
                        pltpu.VMEM((QBLK + tq, GROUP_COLS), BF16)],
        compiler_params=pltpu.CompilerParams(
            dimension_semantics=("arbitrary", "arbitrary", "arbitrary"),
            vmem_limit_bytes=VMEM_LIMIT_BYTES),
        name=f"att_d{dilation}",
    )(hv, pos, invf, sgn, w_in, w_in, w_in)
    return o.reshape(b, s, GROUP_COLS), lse.reshape(b, s, LANES)


def _final_kernel(x_ref, h_ref, gc_ref, o1_ref, o2_ref, o3_ref, l1_ref, l2_ref, l3_ref, ex_ref,
                  wz_ref, wg_ref, wao_ref, wo_ref, gate_ref, fg_ref, out_ref, *, final_norm):
    h = h_ref[0]
    lses = [l1_ref[0], l2_ref[0], l3_ref[0]]
    m = jnp.maximum(jnp.maximum(lses[0], lses[1]), lses[2])
    es = [jnp.exp(l - m) for l in lses]
    inv = 1.0 / (es[0] + es[1] + es[2])
    att = None
    for e, o_ref in zip(es, (o1_ref, o2_ref, o3_ref)):
        w = e * inv
        hi = w.astype(BF16)
        lo = (w - hi.astype(F32)).astype(BF16)
        wf = _dot(hi, ex_ref[...]) + _dot(lo, ex_ref[...])
        term = wf * o_ref[0].astype(F32)
        att = term if att is None else att + term
    ya = _dot((att * _silu(_dot(h, wz_ref[...]))).astype(BF16), wao_ref[...])
    merged = gc_ref[0].astype(F32) + _sigmoid(_dot(h, wg_ref[...])) * ya
    xn = x_ref[0] + gate_ref[0] * _dot(merged.astype(BF16), wo_ref[...])
    if final_norm:
        xn = xn * lax.rsqrt(jnp.mean(xn * xn, axis=-1, keepdims=True) + EPS) * fg_ref[...]
    out_ref[0] = xn


def _merge_and_project(x, h, gc, outs, lses, w_in, w_att_out, w_o, gate, final_g, final_norm, tm):
    b, s, d = x.shape
    tok = lambda width: pl.BlockSpec((1, tm, width), lambda i, j: (i, j, 0))
    head = jnp.arange(LANES)[:, None]
    expand = (head == jnp.arange(GROUP_COLS)[None, :] // HEAD_DIM).astype(BF16)
    z_blk = w_in.shape[1] // GROUP_COLS - 2 * (d // GROUP_COLS) - 1
    g_blk = w_in.shape[1] // d - 1
    return pl.pallas_call(
        functools.partial(_final_kernel, final_norm=final_norm),
        grid=(b, s // tm),
        in_specs=[tok(d), tok(d), tok(d), tok(GROUP_COLS), tok(GROUP_COLS), tok(GROUP_COLS),
                  tok(LANES), tok(LANES), tok(LANES),
                  pl.BlockSpec((LANES, GROUP_COLS), lambda i, j: (0, 0)),
                  pl.BlockSpec((d, GROUP_COLS), lambda i, j: (0, z_blk)),
                  pl.BlockSpec((d, d), lambda i, j: (0, g_blk)),
                  pl.BlockSpec((GROUP_COLS, d), lambda i, j: (0, 0)),
                  pl.BlockSpec((d, d), lambda i, j: (0, 0)),
                  pl.BlockSpec((1, 1, d), lambda i, j: (i, 0, 0)),
                  pl.BlockSpec((1, d), lambda i, j: (0, 0))],
        out_specs=tok(d),
        out_shape=jax.ShapeDtypeStruct((b, s, d), F32),
        compiler_params=pltpu.CompilerParams(dimension_semantics=("arbitrary", "arbitrary"),
                                             vmem_limit_bytes=VMEM_LIMIT_BYTES),
        name="merge_out",
    )(x, h, gc, *outs, *lses, expand, w_in, w_in, w_att_out, w_o, gate.reshape(b, 1, d),
      final_g.reshape(1, d))


def kernel(x, c, positions, norm_g, w_ada, b_ada, w_in, conv_w, conv_b, conv_ln_g, conv_ln_b,
           w_conv_out, w_att_out, w_o, final_g):
    b, s, d = x.shape
    depth = norm_g.shape[0]
    pos_f = positions.astype(F32)
    for layer in range(depth):
        mod = _modulation(c, w_ada[layer], b_ada[layer])
        shift, scale, gate = mod[:, :d], mod[:, d:2 * d], mod[:, 2 * d:]
        w_in_b = w_in[layer].astype(BF16)
        h = _modulated_norm(x, norm_g[layer], scale, shift, ts=1024)
        gc = _conv_branch(h, w_in_b, conv_w[layer], conv_b[layer], conv_ln_g[layer],
                          conv_ln_b[layer], w_conv_out[layer].astype(BF16), tm=512)
        outs, lses = [], []
        for gi, (window, dilation) in enumerate(DILATION_GROUPS):
            assert window == QBLK * dilation
            o, lse = _attention_group(h, pos_f, w_in_b, gi, dilation, tq=512)
            outs.append(o)
            lses.append(lse)
        x = _merge_and_project(x, h, gc, outs, lses, w_in_b, w_att_out[layer].astype(BF16),
                               w_o[layer].astype(BF16), gate, final_g,
                               final_norm=layer == depth - 1, tm=512)
    return x
```

```python
import functools

import jax
import jax.numpy as jnp
from jax import lax
from jax.experimental import pallas as pl
from jax.experimental.pallas import tpu as pltpu

HEAD_DIM = 64
HEADS_PER_GROUP = 8
DILATION_GROUPS = ((128, 1), (512, 4), (2048, 16))
ROT_DIM = HEAD_DIM // 4
ROPE_THETA = 500000.0
CONV_SIZE = 31
EPS = 1e-6
NEG_INF = -1e30

LANES = 128
SUBLANES = 8
VMEM_LIMIT_BYTES = 56 * 1024 * 1024

GROUP_COLS = HEADS_PER_GROUP * HEAD_DIM
HEADS_PER_VREG = LANES // HEAD_DIM
PAIRS = GROUP_COLS // LANES
QBLK = 128
CONV_HALO = 32
CONV_ROWS = 64

F32 = jnp.float32
BF16 = jnp.bfloat16


def _dot(a, b):
    return jnp.dot(a, b, preferred_element_type=F32)


def _dot_nt(a, b):
    return lax.dot_general(a, b, (((1,), (1,)), ((), ())), preferred_element_type=F32)


def _sigmoid(x):
    return 1.0 / (1.0 + jnp.exp(-x))


def _silu(x):
    return x * _sigmoid(x)


def _mod_kernel(c_ref, w_ref, b_ref, o_ref):
    o_ref[...] = jnp.dot(c_ref[...], w_ref[...], preferred_element_type=F32,
                         precision=lax.Precision.HIGHEST) + b_ref[...]


def _modulation(c, w_ada, b_ada):
    b, d = c.shape
    n = w_ada.shape[1]
    return pl.pallas_call(
        _mod_kernel,
        grid=(n // d,),
        in_specs=[pl.BlockSpec((b, d), lambda j: (0, 0)),
                  pl.BlockSpec((d, d), lambda j: (0, j)),
                  pl.BlockSpec((1, d), lambda j: (0, j))],
        out_specs=pl.BlockSpec((b, d), lambda j: (0, j)),
        out_shape=jax.ShapeDtypeStruct((b, n), F32),
        compiler_params=pltpu.CompilerParams(dimension_semantics=("arbitrary",),
                                             vmem_limit_bytes=VMEM_LIMIT_BYTES),
        name="mod",
    )(c, w_ada, b_ada.reshape(1, n))


def _h_kernel(x_ref, g_ref, scale_ref, shift_ref, h_ref, *rest, ratios):
    perm_refs, slabs = rest[:len(ratios)], rest[len(ratios):]
    ts, d = x_ref.shape[1], x_ref.shape[2]
    ncb = d // LANES
    x = x_ref[0]
    y = x * lax.rsqrt(jnp.mean(x * x, axis=-1, keepdims=True) + EPS)
    hf = (y * g_ref[...]) * (1.0 + scale_ref[0]) + shift_ref[0]
    h_ref[0] = hf.astype(BF16)
    for cb in range(ncb):
        slabs[0][0, cb] = hf[:, cb * LANES:(cb + 1) * LANES]
    n_prev, rows = 1, ts
    for lvl, ratio in enumerate(ratios):
        src, out_ref = slabs[lvl], perm_refs[lvl]
        dst = slabs[lvl + 1] if lvl + 1 < len(ratios) else None
        for rp in range(n_prev):
            for q in range(ratio):
                r = q * n_prev + rp
                for cb in range(ncb):
                    part = src[rp, cb, pl.ds(q, rows // ratio, stride=ratio), :]
                    if dst is not None:
                        dst[r, cb] = part
                    out_ref[0, r, :, cb * LANES:(cb + 1) * LANES] = part.astype(BF16)
        n_prev, rows = n_prev * ratio, rows // ratio


def _modulated_norm(x, g, scale, shift, dilations, ts):
    b, s, d = x.shape
    ratios = tuple(dn // dp for dp, dn in zip(dilations[:-1], dilations[1:]))
    assert dilations[0] == 1 and all(dp * r == dn for dp, dn, r in zip(dilations, dilations[1:], ratios))
    ncb = d // LANES
    vec = pl.BlockSpec((1, 1, d), lambda i, j: (i, 0, 0))
    perm_shapes = [jax.ShapeDtypeStruct((b, dl, s // dl, d), BF16) for dl in dilations[1:]]
    perm_specs = [pl.BlockSpec((1, dl, ts // dl, d), lambda i, j: (i, 0, j, 0)) for dl in dilations[1:]]
    slabs = [pltpu.VMEM((dl, ncb, ts // dl, LANES), F32) for dl in dilations[:-1]]
    return pl.pallas_call(
        functools.partial(_h_kernel, ratios=ratios),
        grid=(b, s // ts),
        in_specs=[pl.BlockSpec((1, ts, d), lambda i, j: (i, j, 0)),
                  pl.BlockSpec((1, d), lambda i, j: (0, 0)), vec, vec],
        out_specs=[pl.BlockSpec((1, ts, d), lambda i, j: (i, j, 0))] + perm_specs,
        out_shape=[jax.ShapeDtypeStruct((b, s, d), BF16)] + perm_shapes,
        scratch_shapes=slabs,
        compiler_params=pltpu.CompilerParams(dimension_semantics=("arbitrary", "arbitrary"),
                                             vmem_limit_bytes=VMEM_LIMIT_BYTES),
        name="hnorm",
    )(x, g.reshape(1, d), scale.reshape(b, 1, d), shift.reshape(b, 1, d))


def _conv_kernel(h_ref, wa_ref, wb_ref, wz_ref, wg_ref, cw_ref, cb_ref, lg_ref, lb_ref, wo_ref,
                 out_ref, ubuf, cbuf):
    tm = h_ref.shape[1]
    ncb = ubuf.shape[0]
    h = h_ref[0]

    @pl.when(pl.program_id(1) == 0)
    def _():
        ubuf[:, 0:CONV_HALO, :] = jnp.zeros((ncb, CONV_HALO, LANES), F32)

    @pl.when(pl.program_id(1) > 0)
    def _():
        ubuf[:, 0:CONV_HALO, :] = ubuf[:, tm:tm + CONV_HALO, :]

    u = _dot(h, wa_ref[...]) * _sigmoid(_dot(h, wb_ref[...]))
    for cb in range(ncb):
        ubuf[cb, CONV_HALO:CONV_HALO + tm, :] = u[:, cb * LANES:(cb + 1) * LANES]

    first_tap = CONV_HALO - (CONV_SIZE - 1)

    def row_block(rb, carry):
        r0 = pl.multiple_of(rb * CONV_ROWS, CONV_ROWS)
        for cb in range(ncb):
            cols = slice(cb * LANES, (cb + 1) * LANES)
            acc = jnp.broadcast_to(cb_ref[:, cols], (CONV_ROWS, LANES))
            for t in range(CONV_SIZE):
                acc = acc + ubuf[cb, pl.ds(r0 + first_tap + t, CONV_ROWS), :] * cw_ref[t:t + 1, cols]
            cbuf[pl.ds(r0, CONV_ROWS), cols] = acc
        return carry

    lax.fori_loop(0, tm // CONV_ROWS, row_block, 0)

    c = cbuf[...]
    mu = jnp.mean(c, axis=-1, keepdims=True)
    dlt = c - mu
    var = jnp.mean(dlt * dlt, axis=-1, keepdims=True)
    y = _silu(dlt * lax.rsqrt(var + EPS) * lg_ref[...] + lb_ref[...])
    y = y * _silu(_dot(h, wz_ref[...]))
    yc = _dot(y.astype(BF16), wo_ref[...])
    out_ref[0] = (_sigmoid(_dot(h, wg_ref[...])) * yc).astype(BF16)


def _conv_branch(h, w_in, conv_w, conv_b, ln_g, ln_b, w_out, tm):
    b, s, d = h.shape
    cw = conv_w.shape[1]
    wcol = lambda blk: pl.BlockSpec((d, cw), lambda i, j: (0, blk))
    row = pl.BlockSpec((1, cw), lambda i, j: (0, 0))
    return pl.pallas_call(
        _conv_kernel,
        grid=(b, s // tm),
        in_specs=[pl.BlockSpec((1, tm, d), lambda i, j: (i, j, 0)),
                  wcol(0), wcol(1), wcol(2), wcol(8),
                  pl.BlockSpec((CONV_SIZE, cw), lambda i, j: (0, 0)), row, row, row,
                  pl.BlockSpec((cw, d), lambda i, j: (0, 0))],
        out_specs=pl.BlockSpec((1, tm, d), lambda i, j: (i, j, 0)),
        out_shape=jax.ShapeDtypeStruct((b, s, d), BF16),
        scratch_shapes=[pltpu.VMEM((cw // LANES, CONV_HALO + tm, LANES), F32),
                        pltpu.VMEM((tm, cw), F32)],
        compiler_params=pltpu.CompilerParams(dimension_semantics=("arbitrary", "arbitrary"),
                                             vmem_limit_bytes=VMEM_LIMIT_BYTES),
        name="conv_branch",
    )(h, w_in, w_in, w_in, w_in, conv_w, conv_b.reshape(1, cw), ln_g.reshape(1, cw),
      ln_b.reshape(1, cw), w_out)


def _att_kernel(h_ref, pos_ref, invf_ref, wq_ref, wk_ref, wv_ref, o_ref, lse_ref,
                qbuf, kbuf, vbuf):
    tq = h_ref.shape[2]
    n = pl.program_id(2)
    h = h_ref[0, 0]

    @pl.when(n == 0)
    def _():
        kbuf[0:QBLK, :] = jnp.zeros((QBLK, GROUP_COLS), BF16)
        vbuf[0:QBLK, :] = jnp.zeros((QBLK, GROUP_COLS), BF16)

    @pl.when(n > 0)
    def _():
        kbuf[0:QBLK, :] = kbuf[tq:tq + QBLK, :]
        vbuf[0:QBLK, :] = vbuf[tq:tq + QBLK, :]

    ang = invf_ref[...] * pos_ref[0, 0]
    cos_a, sin_a = jnp.cos(ang), jnp.sin(ang)
    rest = (HEAD_DIM - ROT_DIM, tq)
    cs = jnp.concatenate([cos_a, cos_a, jnp.ones(rest, F32)] * HEADS_PER_VREG, axis=0).T
    sn = jnp.concatenate([-sin_a, sin_a, jnp.zeros(rest, F32)] * HEADS_PER_VREG, axis=0).T
    lane_t = lax.broadcasted_iota(jnp.int32, (tq, LANES), 1)
    low_half = (lane_t & (HEAD_DIM - 1)) < (ROT_DIM // 2)
    head0_t = lane_t < HEAD_DIM

    def rope(t):
        partner = jnp.where(low_half, pltpu.roll(t, LANES - ROT_DIM // 2, 1),
                            pltpu.roll(t, ROT_DIM // 2, 1))
        return t * cs + partner * sn

    q = _dot(h, wq_ref[...])
    k = _dot(h, wk_ref[...])
    for p in range(PAIRS):
        cols = slice(p * LANES, (p + 1) * LANES)
        qr = rope(q[:, cols]) * (HEAD_DIM ** -0.5)
        qbuf[0, :, cols] = jnp.where(head0_t, qr, 0.0).astype(BF16)
        qbuf[1, :, cols] = jnp.where(head0_t, 0.0, qr).astype(BF16)
        kbuf[QBLK:QBLK + tq, cols] = rope(k[:, cols]).astype(BF16)
    vbuf[QBLK:QBLK + tq, :] = _dot(h, wv_ref[...]).astype(BF16)

    row = lax.broadcasted_iota(jnp.int32, (QBLK, 2 * QBLK), 0)
    col = lax.broadcasted_iota(jnp.int32, (QBLK, 2 * QBLK), 1)
    band = ((col < QBLK) & (col >= row)) | ((col >= QBLK) & (col - QBLK <= row))
    band_first = band & ((col >= QBLK) | (n > 0))
    lane = lax.broadcasted_iota(jnp.int32, (QBLK, LANES), 1)

    for qb in range(tq // QBLK):
        rows = slice(qb * QBLK, (qb + 1) * QBLK)
        win = slice(qb * QBLK, (qb + 2) * QBLK)
        valid = band_first if qb == 0 else band
        stats = jnp.zeros((QBLK, LANES), F32)
        for p in range(PAIRS):
            cols = slice(p * LANES, (p + 1) * LANES)
            kw = kbuf[win, cols]
            vw = vbuf[win, cols]
            outs = []
            for hh in range(HEADS_PER_VREG):
                s = jnp.where(valid, _dot_nt(qbuf[hh, rows, cols], kw), NEG_INF)
                m = jnp.max(s, axis=-1, keepdims=True)
                e = jnp.exp(s - m)
                den = jnp.sum(e, axis=-1, keepdims=True)
                outs.append(_dot(e.astype(BF16), vw) * (1.0 / den))
                stats = jnp.where(lane == p * HEADS_PER_VREG + hh, m + jnp.log(den), stats)
            o_ref[0, 0, rows, cols] = jnp.where(lane < HEAD_DIM, outs[0], outs[1]).astype(BF16)
        lse_ref[0, 0, rows, :] = stats


def _attention_group(hp, pos_f, w_in, gi, tq):
    b, dilation, sub_len, d = hp.shape
    pos = pos_f.reshape(b, sub_len, dilation).transpose(0, 2, 1)[:, :, None, :]
    half = ROT_DIM // 2
    invf = (ROPE_THETA ** (-(jnp.arange(half, dtype=F32) * 2.0 / ROT_DIM))).reshape(half, 1)
    col0 = (d * 3) // GROUP_COLS
    n_grp = len(DILATION_GROUPS)
    wspec = lambda blk: pl.BlockSpec((d, GROUP_COLS), lambda i, r, n: (0, blk))
    tile = lambda width: pl.BlockSpec((1, 1, tq, width), lambda i, r, n: (i, r, n, 0))
    return pl.pallas_call(
        _att_kernel,
        grid=(b, dilation, sub_len // tq),
        in_specs=[tile(d),
                  pl.BlockSpec((1, 1, 1, tq), lambda i, r, n: (i, r, 0, n)),
                  pl.BlockSpec((half, 1), lambda i, r, n: (0, 0)),
                  wspec(col0 + gi), wspec(col0 + n_grp + gi), wspec(col0 + 2 * n_grp + gi)],
        out_specs=[tile(GROUP_COLS), tile(LANES)],
        out_shape=[jax.ShapeDtypeStruct((b, dilation, sub_len, GROUP_COLS), BF16),
                   jax.ShapeDtypeStruct((b, dilation, sub_len, LANES), F32)],
        scratch_shapes=[pltpu.VMEM((HEADS_PER_VREG, tq, GROUP_COLS), BF16),
                        pltpu.VMEM((QBLK + tq, GROUP_COLS), BF16),
                        pltpu.VMEM((QBLK + tq, GROUP_COLS), BF16)],
        compiler_params=pltpu.CompilerParams(
            dimension_semantics=("arbitrary", "arbitrary", "arbitrary"),
            vmem_limit_bytes=VMEM_LIMIT_BYTES),
        name=f"att_d{dilation}",
    )(hp, pos, invf, w_in, w_in, w_in)


def _final_kernel(x_ref, h_ref, gc_ref, *rest, dilations, final_norm):
    ng = len(dilations)
    o_refs, l_refs = rest[:ng], rest[ng:2 * ng]
    ex_ref, wz_ref, wg_ref, wao_ref, wo_ref, gate_ref, fg_ref, out_ref = rest[2 * ng:2 * ng + 8]
    slabs = rest[2 * ng + 8:]
    tm = x_ref.shape[1]
    h = h_ref[0]

    lses, o_nat = [], []
    for gi, dl in enumerate(dilations):
        if dl == 1:
            lses.append(l_refs[gi][0, 0])
            o_nat.append([o_refs[gi][0, 0, :, p * LANES:(p + 1) * LANES].astype(F32) for p in range(PAIRS)])
            continue
        o_slab, l_slab = slabs[2 * (gi - 1)], slabs[2 * (gi - 1) + 1]
        for r in range(dl):
            rows = pl.ds(r, tm // dl, stride=dl)
            l_slab[rows, :] = l_refs[gi][0, r]
            for p in range(PAIRS):
                o_slab[p, rows, :] = o_refs[gi][0, r, :, p * LANES:(p + 1) * LANES].astype(F32)
        lses.append(l_slab[...])
        o_nat.append([o_slab[p] for p in range(PAIRS)])

    m = functools.reduce(jnp.maximum, lses)
    es = [jnp.exp(l - m) for l in lses]
    inv = 1.0 / functools.reduce(lambda a, c: a + c, es)
    wfs = []
    for e in es:
        w = e * inv
        hi = w.astype(BF16)
        lo = (w - hi.astype(F32)).astype(BF16)
        wfs.append(_dot(hi, ex_ref[...]) + _dot(lo, ex_ref[...]))
    att = jnp.concatenate(
        [functools.reduce(lambda a, c: a + c,
                          [wf[:, p * LANES:(p + 1) * LANES] * o[p] for wf, o in zip(wfs, o_nat)])
         for p in range(PAIRS)], axis=1)
    ya = _dot((att * _silu(_dot(h, wz_ref[...]))).astype(BF16), wao_ref[...])
    merged = gc_ref[0].astype(F32) + _sigmoid(_dot(h, wg_ref[...])) * ya
    xn = x_ref[0] + gate_ref[0] * _dot(merged.astype(BF16), wo_ref[...])
    if final_norm:
        xn = xn * lax.rsqrt(jnp.mean(xn * xn, axis=-1, keepdims=True) + EPS) * fg_ref[...]
    out_ref[0] = xn


def _merge_and_project(x, h, gc, outs, lses, w_in, w_att_out, w_o, gate, final_g, final_norm, tm):
    b, s, d = x.shape
    dilations = tuple(o.shape[1] for o in outs)
    assert dilations[0] == 1
    tok = lambda width: pl.BlockSpec((1, tm, width), lambda i, j: (i, j, 0))
    perm = lambda dl, width: pl.BlockSpec((1, dl, tm // dl, width), lambda i, j: (i, 0, j, 0))
    head = jnp.arange(LANES)[:, None]
    expand = (head == jnp.arange(GROUP_COLS)[None, :] // HEAD_DIM).astype(BF16)
    z_blk = w_in.shape[1] // GROUP_COLS - 2 * (d // GROUP_COLS) - 1
    g_blk = w_in.shape[1] // d - 1
    slabs = []
    for dl in dilations[1:]:
        slabs += [pltpu.VMEM((PAIRS, tm, LANES), F32), pltpu.VMEM((tm, LANES), F32)]
    return pl.pallas_call(
        functools.partial(_final_kernel, dilations=dilations, final_norm=final_norm),
        grid=(b, s // tm),
        in_specs=[tok(d), tok(d), tok(d)]
                 + [perm(dl, GROUP_COLS) for dl in dilations] + [perm(dl, LANES) for dl in dilations]
                 + [pl.BlockSpec((LANES, GROUP_COLS), lambda i, j: (0, 0)),
                    pl.BlockSpec((d, GROUP_COLS), lambda i, j: (0, z_blk)),
                    pl.BlockSpec((d, d), lambda i, j: (0, g_blk)),
                    pl.BlockSpec((GROUP_COLS, d), lambda i, j: (0, 0)),
                    pl.BlockSpec((d, d), lambda i, j: (0, 0)),
                    pl.BlockSpec((1, 1, d), lambda i, j: (i, 0, 0)),
                    pl.BlockSpec((1, d), lambda i, j: (0, 0))],
        out_specs=tok(d),
        out_shape=jax.ShapeDtypeStruct((b, s, d), F32),
        scratch_shapes=slabs,
        compiler_params=pltpu.CompilerParams(dimension_semantics=("arbitrary", "arbitrary"),
                                             vmem_limit_bytes=VMEM_LIMIT_BYTES),
        name="merge_out",
    )(x, h, gc, *outs, *lses, expand, w_in, w_in, w_att_out, w_o, gate.reshape(b, 1, d),
      final_g.reshape(1, d))


def kernel(x, c, positions, norm_g, w_ada, b_ada, w_in, conv_w, conv_b, conv_ln_g, conv_ln_b,
           w_conv_out, w_att_out, w_o, final_g):
    b, s, d = x.shape
    depth = norm_g.shape[0]
    pos_f = positions.astype(F32)
    dilations = tuple(dl for _, dl in DILATION_GROUPS)
    assert all(window == QBLK * dl for window, dl in DILATION_GROUPS)
    for layer in range(depth):
        mod = _modulation(c, w_ada[layer], b_ada[layer])
        shift, scale, gate = mod[:, :d], mod[:, d:2 * d], mod[:, 2 * d:]
        w_in_b = w_in[layer].astype(BF16)
        h, *h_perm = _modulated_norm(x, norm_g[layer], scale, shift, dilations, ts=1024)
        gc = _conv_branch(h, w_in_b, conv_w[layer], conv_b[layer], conv_ln_g[layer],
                          conv_ln_b[layer], w_conv_out[layer].astype(BF16), tm=512)
        outs, lses = [], []
        for gi, hp in enumerate([h[:, None]] + h_perm):
            o, lse = _attention_group(hp, pos_f, w_in_b, gi, tq=512)
            outs.append(o)
            lses.append(lse)
        x = _merge_and_project(x, h, gc, outs, lses, w_in_b, w_att_out[layer].astype(BF16),
                               w_o[layer].astype(BF16), gate, final_g,
                               final_norm=layer == depth - 1, tm=512)
    return x
```

```python
import functools

import jax
import jax.numpy as jnp
from jax import lax
from jax.experimental import pallas as pl
from jax.experimental.pallas import tpu as pltpu

HEAD_DIM = 64
HEADS_PER_GROUP = 8
DILATION_GROUPS = ((128, 1), (512, 4), (2048, 16))
ROT_DIM = HEAD_DIM // 4
ROPE_THETA = 500000.0
CONV_SIZE = 31
EPS = 1e-6
NEG_INF = -1e30

LANES = 128
SUBLANES = 8
VMEM_LIMIT_BYTES = 56 * 1024 * 1024

GROUP_COLS = HEADS_PER_GROUP * HEAD_DIM
HEADS_PER_VREG = LANES // HEAD_DIM
PAIRS = GROUP_COLS // LANES
QBLK = 128
CONV_HALO = 32
CONV_ROWS = 64
MERGE_SPLIT = 2

F32 = jnp.float32
BF16 = jnp.bfloat16


def _dot(a, b):
    return jnp.dot(a, b, preferred_element_type=F32)


def _dot_nt(a, b):
    return lax.dot_general(a, b, (((1,), (1,)), ((), ())), preferred_element_type=F32)


def _sigmoid(x):
    return 1.0 / (1.0 + jnp.exp(-x))


def _silu(x):
    return x * _sigmoid(x)


def _mod_kernel(c_ref, w_ref, b_ref, o_ref):
    o_ref[...] = jnp.dot(c_ref[...], w_ref[...], preferred_element_type=F32,
                         precision=lax.Precision.HIGHEST) + b_ref[...]


def _modulation(c, w_ada, b_ada):
    b, d = c.shape
    n = w_ada.shape[1]
    return pl.pallas_call(
        _mod_kernel,
        grid=(n // d,),
        in_specs=[pl.BlockSpec((b, d), lambda j: (0, 0)),
                  pl.BlockSpec((d, d), lambda j: (0, j)),
                  pl.BlockSpec((1, d), lambda j: (0, j))],
        out_specs=pl.BlockSpec((b, d), lambda j: (0, j)),
        out_shape=jax.ShapeDtypeStruct((b, n), F32),
        compiler_params=pltpu.CompilerParams(dimension_semantics=("arbitrary",),
                                             vmem_limit_bytes=VMEM_LIMIT_BYTES),
        name="mod",
    )(c, w_ada, b_ada.reshape(1, n))


def _h_kernel(x_ref, g_ref, scale_ref, shift_ref, h_ref, *rest, ratios):
    perm_refs, slabs = rest[:len(ratios)], rest[len(ratios):]
    ts, d = x_ref.shape[1], x_ref.shape[2]
    ncb = d // LANES
    x = x_ref[0]
    y = x * lax.rsqrt(jnp.mean(x * x, axis=-1, keepdims=True) + EPS)
    hf = (y * g_ref[...]) * (1.0 + scale_ref[0]) + shift_ref[0]
    h_ref[0] = hf.astype(BF16)
    for cb in range(ncb):
        slabs[0][0, cb] = hf[:, cb * LANES:(cb + 1) * LANES]
    n_prev, rows = 1, ts
    for lvl, ratio in enumerate(ratios):
        src, out_ref = slabs[lvl], perm_refs[lvl]
        dst = slabs[lvl + 1] if lvl + 1 < len(ratios) else None
        for rp in range(n_prev):
            for q in range(ratio):
                r = q * n_prev + rp
                for cb in range(ncb):
                    part = src[rp, cb, pl.ds(q, rows // ratio, stride=ratio), :]
                    if dst is not None:
                        dst[r, cb] = part
                    out_ref[0, r, :, cb * LANES:(cb + 1) * LANES] = part.astype(BF16)
        n_prev, rows = n_prev * ratio, rows // ratio


def _modulated_norm(x, g, scale, shift, dilations, ts):
    b, s, d = x.shape
    ratios = tuple(dn // dp for dp, dn in zip(dilations[:-1], dilations[1:]))
    assert dilations[0] == 1 and all(dp * r == dn for dp, dn, r in zip(dilations, dilations[1:], ratios))
    ncb = d // LANES
    vec = pl.BlockSpec((1, 1, d), lambda i, j: (i, 0, 0))
    perm_shapes = [jax.ShapeDtypeStruct((b, dl, s // dl, d), BF16) for dl in dilations[1:]]
    perm_specs = [pl.BlockSpec((1, dl, ts // dl, d), lambda i, j: (i, 0, j, 0)) for dl in dilations[1:]]
    slabs = [pltpu.VMEM((dl, ncb, ts // dl, LANES), F32) for dl in dilations[:-1]]
    return pl.pallas_call(
        functools.partial(_h_kernel, ratios=ratios),
        grid=(b, s // ts),
        in_specs=[pl.BlockSpec((1, ts, d), lambda i, j: (i, j, 0)),
                  pl.BlockSpec((1, d), lambda i, j: (0, 0)), vec, vec],
        out_specs=[pl.BlockSpec((1, ts, d), lambda i, j: (i, j, 0))] + perm_specs,
        out_shape=[jax.ShapeDtypeStruct((b, s, d), BF16)] + perm_shapes,
        scratch_shapes=slabs,
        compiler_params=pltpu.CompilerParams(dimension_semantics=("arbitrary", "arbitrary"),
                                             vmem_limit_bytes=VMEM_LIMIT_BYTES),
        name="hnorm",
    )(x, g.reshape(1, d), scale.reshape(b, 1, d), shift.reshape(b, 1, d))


def _conv_kernel(h_ref, wa_ref, wb_ref, wz_ref, wg_ref, cw_ref, cb_ref, lg_ref, lb_ref, wo_ref,
                 out_ref, ubuf, cbuf):
    tm = h_ref.shape[1]
    ncb = ubuf.shape[0]
    h = h_ref[0]

    @pl.when(pl.program_id(1) == 0)
    def _():
        ubuf[:, 0:CONV_HALO, :] = jnp.zeros((ncb, CONV_HALO, LANES), F32)

    @pl.when(pl.program_id(1) > 0)
    def _():
        ubuf[:, 0:CONV_HALO, :] = ubuf[:, tm:tm + CONV_HALO, :]

    u = _dot(h, wa_ref[...]) * _sigmoid(_dot(h, wb_ref[...]))
    for cb in range(ncb):
        ubuf[cb, CONV_HALO:CONV_HALO + tm, :] = u[:, cb * LANES:(cb + 1) * LANES]

    first_tap = CONV_HALO - (CONV_SIZE - 1)

    def row_block(rb, carry):
        r0 = pl.multiple_of(rb * CONV_ROWS, CONV_ROWS)
        for cb in range(ncb):
            cols = slice(cb * LANES, (cb + 1) * LANES)
            acc = jnp.broadcast_to(cb_ref[:, cols], (CONV_ROWS, LANES))
            for t in range(CONV_SIZE):
                acc = acc + ubuf[cb, pl.ds(r0 + first_tap + t, CONV_ROWS), :] * cw_ref[t:t + 1, cols]
            cbuf[pl.ds(r0, CONV_ROWS), cols] = acc
        return carry

    lax.fori_loop(0, tm // CONV_ROWS, row_block, 0)

    c = cbuf[...]
    mu = jnp.mean(c, axis=-1, keepdims=True)
    dlt = c - mu
    var = jnp.mean(dlt * dlt, axis=-1, keepdims=True)
    y = _silu(dlt * lax.rsqrt(var + EPS) * lg_ref[...] + lb_ref[...])
    y = y * _silu(_dot(h, wz_ref[...]))
    yc = _dot(y.astype(BF16), wo_ref[...])
    out_ref[0] = (_sigmoid(_dot(h, wg_ref[...])) * yc).astype(BF16)


def _conv_branch(h, w_in, conv_w, conv_b, ln_g, ln_b, w_out, tm):
    b, s, d = h.shape
    cw = conv_w.shape[1]
    wcol = lambda blk: pl.BlockSpec((d, cw), lambda i, j: (0, blk))
    row = pl.BlockSpec((1, cw), lambda i, j: (0, 0))
    return pl.pallas_call(
        _conv_kernel,
        grid=(b, s // tm),
        in_specs=[pl.BlockSpec((1, tm, d), lambda i, j: (i, j, 0)),
                  wcol(0), wcol(1), wcol(2), wcol(8),
                  pl.BlockSpec((CONV_SIZE, cw), lambda i, j: (0, 0)), row, row, row,
                  pl.BlockSpec((cw, d), lambda i, j: (0, 0))],
        out_specs=pl.BlockSpec((1, tm, d), lambda i, j: (i, j, 0)),
        out_shape=jax.ShapeDtypeStruct((b, s, d), BF16),
        scratch_shapes=[pltpu.VMEM((cw // LANES, CONV_HALO + tm, LANES), F32),
                        pltpu.VMEM((tm, cw), F32)],
        compiler_params=pltpu.CompilerParams(dimension_semantics=("arbitrary", "arbitrary"),
                                             vmem_limit_bytes=VMEM_LIMIT_BYTES),
        name="conv_branch",
    )(h, w_in, w_in, w_in, w_in, conv_w, conv_b.reshape(1, cw), ln_g.reshape(1, cw),
      ln_b.reshape(1, cw), w_out)


def _att_kernel(h_ref, pos_ref, invf_ref, wq_ref, wk_ref, wv_ref, o_ref, lse_ref,
                qbuf, kbuf, vbuf):
    tq = h_ref.shape[2]
    n = pl.program_id(2)
    h = h_ref[0, 0]

    @pl.when(n == 0)
    def _():
        kbuf[0:QBLK, :] = jnp.zeros((QBLK, GROUP_COLS), BF16)
        vbuf[0:QBLK, :] = jnp.zeros((QBLK, GROUP_COLS), BF16)

    @pl.when(n > 0)
    def _():
        kbuf[0:QBLK, :] = kbuf[tq:tq + QBLK, :]
        vbuf[0:QBLK, :] = vbuf[tq:tq + QBLK, :]

    ang = invf_ref[...] * pos_ref[0, 0]
    cos_a, sin_a = jnp.cos(ang), jnp.sin(ang)
    rest = (HEAD_DIM - ROT_DIM, tq)
    cs = jnp.concatenate([cos_a, cos_a, jnp.ones(rest, F32)] * HEADS_PER_VREG, axis=0).T
    sn = jnp.concatenate([-sin_a, sin_a, jnp.zeros(rest, F32)] * HEADS_PER_VREG, axis=0).T
    lane_t = lax.broadcasted_iota(jnp.int32, (tq, LANES), 1)
    low_half = (lane_t & (HEAD_DIM - 1)) < (ROT_DIM // 2)
    head0_t = lane_t < HEAD_DIM

    def rope(t):
        partner = jnp.where(low_half, pltpu.roll(t, LANES - ROT_DIM // 2, 1),
                            pltpu.roll(t, ROT_DIM // 2, 1))
        return t * cs + partner * sn

    q = _dot(h, wq_ref[...])
    k = _dot(h, wk_ref[...])
    for p in range(PAIRS):
        cols = slice(p * LANES, (p + 1) * LANES)
        qr = rope(q[:, cols]) * (HEAD_DIM ** -0.5)
        qbuf[0, :, cols] = jnp.where(head0_t, qr, 0.0).astype(BF16)
        qbuf[1, :, cols] = jnp.where(head0_t, 0.0, qr).astype(BF16)
        kbuf[QBLK:QBLK + tq, cols] = rope(k[:, cols]).astype(BF16)
    vbuf[QBLK:QBLK + tq, :] = _dot(h, wv_ref[...]).astype(BF16)

    row = lax.broadcasted_iota(jnp.int32, (QBLK, 2 * QBLK), 0)
    col = lax.broadcasted_iota(jnp.int32, (QBLK, 2 * QBLK), 1)
    band = ((col < QBLK) & (col >= row)) | ((col >= QBLK) & (col - QBLK <= row))
    band_first = band & ((col >= QBLK) | (n > 0))
    lane = lax.broadcasted_iota(jnp.int32, (QBLK, LANES), 1)

    for qb in range(tq // QBLK):
        rows = slice(qb * QBLK, (qb + 1) * QBLK)
        win = slice(qb * QBLK, (qb + 2) * QBLK)
        valid = band_first if qb == 0 else band
        stats = jnp.zeros((QBLK, LANES), F32)
        for p in range(PAIRS):
            cols = slice(p * LANES, (p + 1) * LANES)
            kw = kbuf[win, cols]
            vw = vbuf[win, cols]
            outs = []
            for hh in range(HEADS_PER_VREG):
                s = jnp.where(valid, _dot_nt(qbuf[hh, rows, cols], kw), NEG_INF)
                m = jnp.max(s, axis=-1, keepdims=True)
                e = jnp.exp(s - m)
                den = jnp.sum(e, axis=-1, keepdims=True)
                outs.append(_dot(e.astype(BF16), vw))
                head = p * HEADS_PER_VREG + hh
                stats = jnp.where(lane == head, m, jnp.where(lane == HEADS_PER_GROUP + head, den, stats))
            o_ref[0, 0, rows, cols] = jnp.where(lane < HEAD_DIM, outs[0], outs[1]).astype(BF16)
        lse_ref[0, 0, rows, :] = stats


def _attention_group(hp, pos_f, w_in, gi, tq):
    b, dilation, sub_len, d = hp.shape
    pos = pos_f.reshape(b, sub_len, dilation).transpose(0, 2, 1)[:, :, None, :]
    half = ROT_DIM // 2
    invf = (ROPE_THETA ** (-(jnp.arange(half, dtype=F32) * 2.0 / ROT_DIM))).reshape(half, 1)
    col0 = (d * 3) // GROUP_COLS
    n_grp = len(DILATION_GROUPS)
    wspec = lambda blk: pl.BlockSpec((d, GROUP_COLS), lambda i, r, n: (0, blk))
    tile = lambda width: pl.BlockSpec((1, 1, tq, width), lambda i, r, n: (i, r, n, 0))
    return pl.pallas_call(
        _att_kernel,
        grid=(b, dilation, sub_len // tq),
        in_specs=[tile(d),
                  pl.BlockSpec((1, 1, 1, tq), lambda i, r, n: (i, r, 0, n)),
                  pl.BlockSpec((half, 1), lambda i, r, n: (0, 0)),
                  wspec(col0 + gi), wspec(col0 + n_grp + gi), wspec(col0 + 2 * n_grp + gi)],
        out_specs=[tile(GROUP_COLS), tile(LANES)],
        out_shape=[jax.ShapeDtypeStruct((b, dilation, sub_len, GROUP_COLS), BF16),
                   jax.ShapeDtypeStruct((b, dilation, sub_len, LANES), F32)],
        scratch_shapes=[pltpu.VMEM((HEADS_PER_VREG, tq, GROUP_COLS), BF16),
                        pltpu.VMEM((QBLK + tq, GROUP_COLS), BF16),
                        pltpu.VMEM((QBLK + tq, GROUP_COLS), BF16)],
        compiler_params=pltpu.CompilerParams(
            dimension_semantics=("arbitrary", "arbitrary", "arbitrary"),
            vmem_limit_bytes=VMEM_LIMIT_BYTES),
        name=f"att_d{dilation}",
    )(hp, pos, invf, w_in, w_in, w_in)


def _final_kernel(x_ref, h_ref, gc_ref, *rest, dilations, final_norm):
    ng = len(dilations)
    o_refs, l_refs = rest[:ng], rest[ng:2 * ng]
    ex_ref, wz_ref, wg_ref, wao_ref, wo_ref, gate_ref, fg_ref, out_ref = rest[2 * ng:2 * ng + 8]
    slabs = rest[2 * ng + 8:]
    tm = x_ref.shape[1]

    for gi, dl in enumerate(dilations):
        if dl == 1:
            continue
        o_slab, l_slab = slabs[2 * (gi - 1)], slabs[2 * (gi - 1) + 1]
        for r in range(dl):
            rows = pl.ds(r, tm // dl, stride=dl)
            l_slab[rows, :] = l_refs[gi][0, r]
            for p in range(PAIRS):
                o_slab[p, rows, :] = o_refs[gi][0, r, :, p * LANES:(p + 1) * LANES].astype(F32)

    sub = tm // MERGE_SPLIT
    head_lane = lax.broadcasted_iota(jnp.int32, (sub, LANES), 1) < HEADS_PER_GROUP

    def rows_of(sb):
        return slice(sb * sub, (sb + 1) * sub)

    def stage_gates(sb, st):
        h = h_ref[0, rows_of(sb), :]
        st["z"] = _dot(h, wz_ref[...])
        st["g"] = _dot(h, wg_ref[...])

    def stage_merge(sb, st):
        rows = rows_of(sb)
        lses, o_nat = [], []
        for gi, dl in enumerate(dilations):
            if dl == 1:
                lses.append(l_refs[gi][0, 0, rows, :])
                o_nat.append([o_refs[gi][0, 0, rows, p * LANES:(p + 1) * LANES].astype(F32)
                              for p in range(PAIRS)])
            else:
                o_slab, l_slab = slabs[2 * (gi - 1)], slabs[2 * (gi - 1) + 1]
                lses.append(l_slab[rows, :])
                o_nat.append([o_slab[p, rows, :] for p in range(PAIRS)])
        m = functools.reduce(jnp.maximum, lses)
        es = [jnp.exp(l - m) for l in lses]
        dens = [pltpu.roll(l, LANES - HEADS_PER_GROUP, 1) for l in lses]
        z = functools.reduce(lambda a, c: a + c, [e * dn for e, dn in zip(es, dens)])
        inv = 1.0 / jnp.where(head_lane, z, 1.0)
        wfs = []
        for e in es:
            w = jnp.where(head_lane, e * inv, 0.0)
            wfs.append(_dot(w.astype(BF16), ex_ref[...]))
        att = jnp.concatenate(
            [functools.reduce(lambda a, c: a + c,
                              [wf[:, p * LANES:(p + 1) * LANES] * o[p] for wf, o in zip(wfs, o_nat)])
             for p in range(PAIRS)], axis=1)
        st["a"] = (att * _silu(st.pop("z"))).astype(BF16)

    def stage_att_out(sb, st):
        st["ya"] = _dot(st.pop("a"), wao_ref[...])

    def stage_gate_merge(sb, st):
        merged = gc_ref[0, rows_of(sb), :].astype(F32) + _sigmoid(st.pop("g")) * st.pop("ya")
        st["merged"] = merged.astype(BF16)

    def stage_w_o(sb, st):
        st["r"] = _dot(st.pop("merged"), wo_ref[...])

    def stage_residual(sb, st):
        xn = x_ref[0, rows_of(sb), :] + gate_ref[0] * st.pop("r")
        if final_norm:
            xn = xn * lax.rsqrt(jnp.mean(xn * xn, axis=-1, keepdims=True) + EPS) * fg_ref[...]
        out_ref[0, rows_of(sb), :] = xn

    stages = (stage_gates, stage_merge, stage_att_out, stage_gate_merge, stage_w_o, stage_residual)
    state = [dict() for _ in range(MERGE_SPLIT)]
    for tick in range(len(stages) + MERGE_SPLIT - 1):
        for sb in range(MERGE_SPLIT):
            if 0 <= tick - sb < len(stages):
                stages[tick - sb](sb, state[sb])


def _merge_and_project(x, h, gc, outs, lses, w_in, w_att_out, w_o, gate, final_g, final_norm, tm):
    b, s, d = x.shape
    dilations = tuple(o.shape[1] for o in outs)
    assert dilations[0] == 1
    tok = lambda width: pl.BlockSpec((1, tm, width), lambda i, j: (i, j, 0))
    perm = lambda dl, width: pl.BlockSpec((1, dl, tm // dl, width), lambda i, j: (i, 0, j, 0))
    head = jnp.arange(LANES)[:, None]
    expand = (head == jnp.arange(GROUP_COLS)[None, :] // HEAD_DIM).astype(BF16)
    z_blk = w_in.shape[1] // GROUP_COLS - 2 * (d // GROUP_COLS) - 1
    g_blk = w_in.shape[1] // d - 1
    slabs = []
    for dl in dilations[1:]:
        slabs += [pltpu.VMEM((PAIRS, tm, LANES), F32), pltpu.VMEM((tm, LANES), F32)]
    return pl.pallas_call(
        functools.partial(_final_kernel, dilations=dilations, final_norm=final_norm),
        grid=(b, s // tm),
        in_specs=[tok(d), tok(d), tok(d)]
                 + [perm(dl, GROUP_COLS) for dl in dilations] + [perm(dl, LANES) for dl in dilations]
                 + [pl.BlockSpec((LANES, GROUP_COLS), lambda i, j: (0, 0)),
                    pl.BlockSpec((d, GROUP_COLS), lambda i, j: (0, z_blk)),
                    pl.BlockSpec((d, d), lambda i, j: (0, g_blk)),
                    pl.BlockSpec((GROUP_COLS, d), lambda i, j: (0, 0)),
                    pl.BlockSpec((d, d), lambda i, j: (0, 0)),
                    pl.BlockSpec((1, 1, d), lambda i, j: (i, 0, 0)),
                    pl.BlockSpec((1, d), lambda i, j: (0, 0))],
        out_specs=tok(d),
        out_shape=jax.ShapeDtypeStruct((b, s, d), F32),
        scratch_shapes=slabs,
        compiler_params=pltpu.CompilerParams(dimension_semantics=("arbitrary", "arbitrary"),
                                             vmem_limit_bytes=VMEM_LIMIT_BYTES),
        name="merge_out",
    )(x, h, gc, *outs, *lses, expand, w_in, w_in, w_att_out, w_o, gate.reshape(b, 1, d),
      final_g.reshape(1, d))


def kernel(x, c, positions, norm_g, w_ada, b_ada, w_in, conv_w, conv_b, conv_ln_g, conv_ln_b,
           w_conv_out, w_att_out, w_o, final_g):
    b, s, d = x.shape
    depth = norm_g.shape[0]
    pos_f = positions.astype(F32)
    dilations = tuple(dl for _, dl in DILATION_GROUPS)
    assert all(window == QBLK * dl for window, dl in DILATION_GROUPS)
    for layer in range(depth):
        mod = _modulation(c, w_ada[layer], b_ada[layer])
        shift, scale, gate = mod[:, :d], mod[:, d:2 * d], mod[:, 2 * d:]
        w_in_b = w_in[layer].astype(BF16)
        h, *h_perm = _modulated_norm(x, norm_g[layer], scale, shift, dilations, ts=1024)
        gc = _conv_branch(h, w_in_b, conv_w[layer], conv_b[layer], conv_ln_g[layer],
                          conv_ln_b[layer], w_conv_out[layer].astype(BF16), tm=512)
        outs, lses = [], []
        for gi, hp in enumerate([h[:, None]] + h_perm):
            o, lse = _attention_group(hp, pos_f, w_in_b, gi, tq=512)
            outs.append(o)
            lses.append(lse)
        x = _merge_and_project(x, h, gc, outs, lses, w_in_b, w_att_out[layer].astype(BF16),
                               w_o[layer].astype(BF16), gate, final_g,
                               final_norm=layer == depth - 1, tm=512)
    return x
```

```python
import functools

import jax
import jax.numpy as jnp
from jax import lax
from jax.experimental import pallas as pl
from jax.experimental.pallas import tpu as pltpu

HEAD_DIM = 64
HEADS_PER_GROUP = 8
DILATION_GROUPS = ((128, 1), (512, 4), (2048, 16))
ROT_DIM = HEAD_DIM // 4
ROPE_THETA = 500000.0
CONV_SIZE = 31
EPS = 1e-6
NEG_INF = -1e30

LANES = 128
SUBLANES = 8
VMEM_LIMIT_BYTES = 56 * 1024 * 1024

GROUP_COLS = HEADS_PER_GROUP * HEAD_DIM
HEADS_PER_VREG = LANES // HEAD_DIM
PAIRS = GROUP_COLS // LANES
QBLK = 128
CONV_HALO = 32
CONV_ROWS = 64
MERGE_SPLIT = 2

F32 = jnp.float32
BF16 = jnp.bfloat16


def _dot(a, b):
    return jnp.dot(a, b, preferred_element_type=F32)


def _dot_nt(a, b):
    return lax.dot_general(a, b, (((1,), (1,)), ((), ())), preferred_element_type=F32)


def _sigmoid(x):
    return 1.0 / (1.0 + jnp.exp(-x))


def _silu(x):
    return x * _sigmoid(x)


def _mod_kernel(c_ref, w_ref, b_ref, o_ref):
    o_ref[...] = jnp.dot(c_ref[...], w_ref[...], preferred_element_type=F32,
                         precision=lax.Precision.HIGHEST) + b_ref[...]


def _modulation(c, w_ada, b_ada):
    b, d = c.shape
    n = w_ada.shape[1]
    return pl.pallas_call(
        _mod_kernel,
        grid=(n // d,),
        in_specs=[pl.BlockSpec((b, d), lambda j: (0, 0)),
                  pl.BlockSpec((d, d), lambda j: (0, j)),
                  pl.BlockSpec((1, d), lambda j: (0, j))],
        out_specs=pl.BlockSpec((b, d), lambda j: (0, j)),
        out_shape=jax.ShapeDtypeStruct((b, n), F32),
        compiler_params=pltpu.CompilerParams(dimension_semantics=("arbitrary",),
                                             vmem_limit_bytes=VMEM_LIMIT_BYTES),
        name="mod",
    )(c, w_ada, b_ada.reshape(1, n))


def _h_kernel(x_ref, g_ref, scale_ref, shift_ref, h_ref, *rest, ratios):
    perm_refs, slabs = rest[:len(ratios)], rest[len(ratios):]
    ts, d = x_ref.shape[1], x_ref.shape[2]
    ncb = d // LANES
    x = x_ref[0]
    y = x * lax.rsqrt(jnp.mean(x * x, axis=-1, keepdims=True) + EPS)
    hf = (y * g_ref[...]) * (1.0 + scale_ref[0]) + shift_ref[0]
    h_ref[0] = hf.astype(BF16)
    for cb in range(ncb):
        slabs[0][0, cb] = hf[:, cb * LANES:(cb + 1) * LANES]
    n_prev, rows = 1, ts
    for lvl, ratio in enumerate(ratios):
        src, out_ref = slabs[lvl], perm_refs[lvl]
        dst = slabs[lvl + 1] if lvl + 1 < len(ratios) else None
        for rp in range(n_prev):
            for q in range(ratio):
                r = q * n_prev + rp
                for cb in range(ncb):
                    part = src[rp, cb, pl.ds(q, rows // ratio, stride=ratio), :]
                    if dst is not None:
                        dst[r, cb] = part
                    out_ref[0, r, :, cb * LANES:(cb + 1) * LANES] = part.astype(BF16)
        n_prev, rows = n_prev * ratio, rows // ratio


def _modulated_norm(x, g, scale, shift, dilations, ts):
    b, s, d = x.shape
    ratios = tuple(dn // dp for dp, dn in zip(dilations[:-1], dilations[1:]))
    assert dilations[0] == 1 and all(dp * r == dn for dp, dn, r in zip(dilations, dilations[1:], ratios))
    ncb = d // LANES
    vec = pl.BlockSpec((1, 1, d), lambda i, j: (i, 0, 0))
    perm_shapes = [jax.ShapeDtypeStruct((b, dl, s // dl, d), BF16) for dl in dilations[1:]]
    perm_specs = [pl.BlockSpec((1, dl, ts // dl, d), lambda i, j: (i, 0, j, 0)) for dl in dilations[1:]]
    slabs = [pltpu.VMEM((dl, ncb, ts // dl, LANES), F32) for dl in dilations[:-1]]
    return pl.pallas_call(
        functools.partial(_h_kernel, ratios=ratios),
        grid=(b, s // ts),
        in_specs=[pl.BlockSpec((1, ts, d), lambda i, j: (i, j, 0)),
                  pl.BlockSpec((1, d), lambda i, j: (0, 0)), vec, vec],
        out_specs=[pl.BlockSpec((1, ts, d), lambda i, j: (i, j, 0))] + perm_specs,
        out_shape=[jax.ShapeDtypeStruct((b, s, d), BF16)] + perm_shapes,
        scratch_shapes=slabs,
        compiler_params=pltpu.CompilerParams(dimension_semantics=("arbitrary", "arbitrary"),
                                             vmem_limit_bytes=VMEM_LIMIT_BYTES),
        name="hnorm",
    )(x, g.reshape(1, d), scale.reshape(b, 1, d), shift.reshape(b, 1, d))


def _conv_kernel(h_ref, wa_ref, wb_ref, wz_ref, wg_ref, cw_ref, cb_ref, lg_ref, lb_ref, wo_ref,
                 out_ref, ubuf, cbuf):
    tm = h_ref.shape[1]
    ncb = ubuf.shape[0]
    h = h_ref[0]

    @pl.when(pl.program_id(1) == 0)
    def _():
        ubuf[:, 0:CONV_HALO, :] = jnp.zeros((ncb, CONV_HALO, LANES), F32)

    @pl.when(pl.program_id(1) > 0)
    def _():
        ubuf[:, 0:CONV_HALO, :] = ubuf[:, tm:tm + CONV_HALO, :]

    u = _dot(h, wa_ref[...]) * _sigmoid(_dot(h, wb_ref[...]))
    for cb in range(ncb):
        ubuf[cb, CONV_HALO:CONV_HALO + tm, :] = u[:, cb * LANES:(cb + 1) * LANES]

    first_tap = CONV_HALO - (CONV_SIZE - 1)

    def row_block(rb, carry):
        r0 = pl.multiple_of(rb * CONV_ROWS, CONV_ROWS)
        for cb in range(ncb):
            cols = slice(cb * LANES, (cb + 1) * LANES)
            acc = jnp.broadcast_to(cb_ref[:, cols], (CONV_ROWS, LANES))
            for t in range(CONV_SIZE):
                acc = acc + ubuf[cb, pl.ds(r0 + first_tap + t, CONV_ROWS), :] * cw_ref[t:t + 1, cols]
            cbuf[pl.ds(r0, CONV_ROWS), cols] = acc
        return carry

    lax.fori_loop(0, tm // CONV_ROWS, row_block, 0)

    c = cbuf[...]
    mu = jnp.mean(c, axis=-1, keepdims=True)
    dlt = c - mu
    var = jnp.mean(dlt * dlt, axis=-1, keepdims=True)
    y = _silu(dlt * lax.rsqrt(var + EPS) * lg_ref[...] + lb_ref[...])
    y = y * _silu(_dot(h, wz_ref[...]))
    yc = _dot(y.astype(BF16), wo_ref[...])
    out_ref[0] = (_sigmoid(_dot(h, wg_ref[...])) * yc).astype(BF16)


def _conv_branch(h, w_in, conv_w, conv_b, ln_g, ln_b, w_out, tm):
    b, s, d = h.shape
    cw = conv_w.shape[1]
    wcol = lambda blk: pl.BlockSpec((d, cw), lambda i, j: (0, blk))
    row = pl.BlockSpec((1, cw), lambda i, j: (0, 0))
    return pl.pallas_call(
        _conv_kernel,
        grid=(b, s // tm),
        in_specs=[pl.BlockSpec((1, tm, d), lambda i, j: (i, j, 0)),
                  wcol(0), wcol(1), wcol(2), wcol(8),
                  pl.BlockSpec((CONV_SIZE, cw), lambda i, j: (0, 0)), row, row, row,
                  pl.BlockSpec((cw, d), lambda i, j: (0, 0))],
        out_specs=pl.BlockSpec((1, tm, d), lambda i, j: (i, j, 0)),
        out_shape=jax.ShapeDtypeStruct((b, s, d), BF16),
        scratch_shapes=[pltpu.VMEM((cw // LANES, CONV_HALO + tm, LANES), F32),
                        pltpu.VMEM((tm, cw), F32)],
        compiler_params=pltpu.CompilerParams(dimension_semantics=("arbitrary", "arbitrary"),
                                             vmem_limit_bytes=VMEM_LIMIT_BYTES),
        name="conv_branch",
    )(h, w_in, w_in, w_in, w_in, conv_w, conv_b.reshape(1, cw), ln_g.reshape(1, cw),
      ln_b.reshape(1, cw), w_out)


def _att_kernel(h_ref, pos_ref, invf_ref, wq_ref, wk_ref, wv_ref, o_ref, lse_ref,
                qbuf, kbuf, vbuf):
    n_sub, tq, d = h_ref.shape[1], h_ref.shape[2], h_ref.shape[3]
    rows_all = n_sub * tq
    n = pl.program_id(2)
    h = h_ref[0].reshape(rows_all, d)

    @pl.when(n == 0)
    def _():
        kbuf[0:QBLK, :] = jnp.zeros((QBLK, GROUP_COLS), BF16)
        vbuf[0:QBLK, :] = jnp.zeros((QBLK, GROUP_COLS), BF16)

    @pl.when(n > 0)
    def _():
        kbuf[0:QBLK, :] = kbuf[rows_all:rows_all + QBLK, :]
        vbuf[0:QBLK, :] = vbuf[rows_all:rows_all + QBLK, :]

    pos = jnp.concatenate([pos_ref[0, r] for r in range(n_sub)], axis=1)
    ang = invf_ref[...] * pos
    cos_a, sin_a = jnp.cos(ang), jnp.sin(ang)
    rest = (HEAD_DIM - ROT_DIM, rows_all)
    cs = jnp.concatenate([cos_a, cos_a, jnp.ones(rest, F32)] * HEADS_PER_VREG, axis=0).T
    sn = jnp.concatenate([-sin_a, sin_a, jnp.zeros(rest, F32)] * HEADS_PER_VREG, axis=0).T
    lane_t = lax.broadcasted_iota(jnp.int32, (rows_all, LANES), 1)
    low_half = (lane_t & (HEAD_DIM - 1)) < (ROT_DIM // 2)
    head0_t = lane_t < HEAD_DIM

    def rope(t):
        partner = jnp.where(low_half, pltpu.roll(t, LANES - ROT_DIM // 2, 1),
                            pltpu.roll(t, ROT_DIM // 2, 1))
        return t * cs + partner * sn

    q = _dot(h, wq_ref[...])
    k = _dot(h, wk_ref[...])
    for p in range(PAIRS):
        cols = slice(p * LANES, (p + 1) * LANES)
        qr = rope(q[:, cols]) * (HEAD_DIM ** -0.5)
        qbuf[0, :, cols] = jnp.where(head0_t, qr, 0.0).astype(BF16)
        qbuf[1, :, cols] = jnp.where(head0_t, 0.0, qr).astype(BF16)
        kbuf[QBLK:QBLK + rows_all, cols] = rope(k[:, cols]).astype(BF16)
    vbuf[QBLK:QBLK + rows_all, :] = _dot(h, wv_ref[...]).astype(BF16)

    row = lax.broadcasted_iota(jnp.int32, (QBLK, 2 * QBLK), 0)
    col = lax.broadcasted_iota(jnp.int32, (QBLK, 2 * QBLK), 1)
    band = ((col < QBLK) & (col >= row)) | ((col >= QBLK) & (col - QBLK <= row))
    band_new_sub = band & (col >= QBLK)
    band_first = band & ((col >= QBLK) | (n > 0))
    lane = lax.broadcasted_iota(jnp.int32, (QBLK, LANES), 1)

    blocks_per_sub = tq // QBLK
    for qb in range(rows_all // QBLK):
        r, qs = qb // blocks_per_sub, qb % blocks_per_sub
        rows = slice(qb * QBLK, (qb + 1) * QBLK)
        win = slice(qb * QBLK, (qb + 2) * QBLK)
        valid = band if qs else (band_new_sub if r else band_first)
        stats = jnp.zeros((QBLK, LANES), F32)
        for p in range(PAIRS):
            cols = slice(p * LANES, (p + 1) * LANES)
            kw = kbuf[win, cols]
            vw = vbuf[win, cols]
            outs = []
            for hh in range(HEADS_PER_VREG):
                s = jnp.where(valid, _dot_nt(qbuf[hh, rows, cols], kw), NEG_INF)
                m = jnp.max(s, axis=-1, keepdims=True)
                e = jnp.exp(s - m)
                den = jnp.sum(e, axis=-1, keepdims=True)
                outs.append(_dot(e.astype(BF16), vw))
                head = p * HEADS_PER_VREG + hh
                stats = jnp.where(lane == head, m, jnp.where(lane == HEADS_PER_GROUP + head, den, stats))
            sub_rows = slice(qs * QBLK, (qs + 1) * QBLK)
            o_ref[0, r, sub_rows, cols] = jnp.where(lane < HEAD_DIM, outs[0], outs[1]).astype(BF16)
        lse_ref[0, r, slice(qs * QBLK, (qs + 1) * QBLK), :] = stats


def _attention_group(hp, pos_f, w_in, gi, rows_per_step):
    b, dilation, sub_len, d = hp.shape
    tq = min(rows_per_step, sub_len)
    n_sub = min(rows_per_step // tq, dilation)
    assert n_sub == 1 or tq == sub_len
    pos = pos_f.reshape(b, sub_len, dilation).transpose(0, 2, 1)[:, :, None, :]
    half = ROT_DIM // 2
    invf = (ROPE_THETA ** (-(jnp.arange(half, dtype=F32) * 2.0 / ROT_DIM))).reshape(half, 1)
    col0 = (d * 3) // GROUP_COLS
    n_grp = len(DILATION_GROUPS)
    wspec = lambda blk: pl.BlockSpec((d, GROUP_COLS), lambda i, r, n: (0, blk))
    tile = lambda width: pl.BlockSpec((1, n_sub, tq, width), lambda i, r, n: (i, r, n, 0))
    return pl.pallas_call(
        _att_kernel,
        grid=(b, dilation // n_sub, sub_len // tq),
        in_specs=[tile(d),
                  pl.BlockSpec((1, n_sub, 1, tq), lambda i, r, n: (i, r, 0, n)),
                  pl.BlockSpec((half, 1), lambda i, r, n: (0, 0)),
                  wspec(col0 + gi), wspec(col0 + n_grp + gi), wspec(col0 + 2 * n_grp + gi)],
        out_specs=[tile(GROUP_COLS), tile(LANES)],
        out_shape=[jax.ShapeDtypeStruct((b, dilation, sub_len, GROUP_COLS), BF16),
                   jax.ShapeDtypeStruct((b, dilation, sub_len, LANES), F32)],
        scratch_shapes=[pltpu.VMEM((HEADS_PER_VREG, n_sub * tq, GROUP_COLS), BF16),
                        pltpu.VMEM((QBLK + n_sub * tq, GROUP_COLS), BF16),
                        pltpu.VMEM((QBLK + n_sub * tq, GROUP_COLS), BF16)],
        compiler_params=pltpu.CompilerParams(
            dimension_semantics=("arbitrary", "arbitrary", "arbitrary"),
            vmem_limit_bytes=VMEM_LIMIT_BYTES),
        name=f"att_d{dilation}",
    )(hp, pos, invf, w_in, w_in, w_in)


def _final_kernel(x_ref, h_ref, gc_ref, *rest, dilations, final_norm):
    ng = len(dilations)
    o_refs, l_refs = rest[:ng], rest[ng:2 * ng]
    ex_ref, wz_ref, wg_ref, wao_ref, wo_ref, gate_ref, fg_ref, out_ref = rest[2 * ng:2 * ng + 8]
    slabs = rest[2 * ng + 8:]
    tm = x_ref.shape[1]

    for gi, dl in enumerate(dilations):
        if dl == 1:
            continue
        o_slab, l_slab = slabs[2 * (gi - 1)], slabs[2 * (gi - 1) + 1]
        for r in range(dl):
            rows = pl.ds(r, tm // dl, stride=dl)
            l_slab[rows, :] = l_refs[gi][0, r]
            for p in range(PAIRS):
                o_slab[p, rows, :] = o_refs[gi][0, r, :, p * LANES:(p + 1) * LANES].astype(F32)

    sub = tm // MERGE_SPLIT
    head_lane = lax.broadcasted_iota(jnp.int32, (sub, LANES), 1) < HEADS_PER_GROUP

    def rows_of(sb):
        return slice(sb * sub, (sb + 1) * sub)

    def stage_gates(sb, st):
        h = h_ref[0, rows_of(sb), :]
        st["z"] = _dot(h, wz_ref[...])
        st["g"] = _dot(h, wg_ref[...])

    def stage_merge(sb, st):
        rows = rows_of(sb)
        lses, o_nat = [], []
        for gi, dl in enumerate(dilations):
            if dl == 1:
                lses.append(l_refs[gi][0, 0, rows, :])
                o_nat.append([o_refs[gi][0, 0, rows, p * LANES:(p + 1) * LANES].astype(F32)
                              for p in range(PAIRS)])
            else:
                o_slab, l_slab = slabs[2 * (gi - 1)], slabs[2 * (gi - 1) + 1]
                lses.append(l_slab[rows, :])
                o_nat.append([o_slab[p, rows, :] for p in range(PAIRS)])
        m = functools.reduce(jnp.maximum, lses)
        es = [jnp.exp(l - m) for l in lses]
        dens = [pltpu.roll(l, LANES - HEADS_PER_GROUP, 1) for l in lses]
        z = functools.reduce(lambda a, c: a + c, [e * dn for e, dn in zip(es, dens)])
        inv = 1.0 / jnp.where(head_lane, z, 1.0)
        wfs = []
        for e in es:
            w = jnp.where(head_lane, e * inv, 0.0)
            wfs.append(_dot(w.astype(BF16), ex_ref[...]))
        att = jnp.concatenate(
            [functools.reduce(lambda a, c: a + c,
                              [wf[:, p * LANES:(p + 1) * LANES] * o[p] for wf, o in zip(wfs, o_nat)])
             for p in range(PAIRS)], axis=1)
        st["a"] = (att * _silu(st.pop("z"))).astype(BF16)

    def stage_att_out(sb, st):
        st["ya"] = _dot(st.pop("a"), wao_ref[...])

    def stage_gate_merge(sb, st):
        merged = gc_ref[0, rows_of(sb), :].astype(F32) + _sigmoid(st.pop("g")) * st.pop("ya")
        st["merged"] = merged.astype(BF16)

    def stage_w_o(sb, st):
        st["r"] = _dot(st.pop("merged"), wo_ref[...])

    def stage_residual(sb, st):
        xn = x_ref[0, rows_of(sb), :] + gate_ref[0] * st.pop("r")
        if final_norm:
            xn = xn * lax.rsqrt(jnp.mean(xn * xn, axis=-1, keepdims=True) + EPS) * fg_ref[...]
        out_ref[0, rows_of(sb), :] = xn

    stages = (stage_gates, stage_merge, stage_att_out, stage_gate_merge, stage_w_o, stage_residual)
    state = [dict() for _ in range(MERGE_SPLIT)]
    for tick in range(len(stages) + MERGE_SPLIT - 1):
        for sb in range(MERGE_SPLIT):
            if 0 <= tick - sb < len(stages):
                stages[tick - sb](sb, state[sb])


def _merge_and_project(x, h, gc, outs, lses, w_in, w_att_out, w_o, gate, final_g, final_norm, tm):
    b, s, d = x.shape
    dilations = tuple(o.shape[1] for o in outs)
    assert dilations[0] == 1
    tok = lambda width: pl.BlockSpec((1, tm, width), lambda i, j: (i, j, 0))
    perm = lambda dl, width: pl.BlockSpec((1, dl, tm // dl, width), lambda i, j: (i, 0, j, 0))
    head = jnp.arange(LANES)[:, None]
    expand = (head == jnp.arange(GROUP_COLS)[None, :] // HEAD_DIM).astype(BF16)
    z_blk = w_in.shape[1] // GROUP_COLS - 2 * (d // GROUP_COLS) - 1
    g_blk = w_in.shape[1] // d - 1
    slabs = []
    for dl in dilations[1:]:
        slabs += [pltpu.VMEM((PAIRS, tm, LANES), F32), pltpu.VMEM((tm, LANES), F32)]
    return pl.pallas_call(
        functools.partial(_final_kernel, dilations=dilations, final_norm=final_norm),
        grid=(b, s // tm),
        in_specs=[tok(d), tok(d), tok(d)]
                 + [perm(dl, GROUP_COLS) for dl in dilations] + [perm(dl, LANES) for dl in dilations]
                 + [pl.BlockSpec((LANES, GROUP_COLS), lambda i, j: (0, 0)),
                    pl.BlockSpec((d, GROUP_COLS), lambda i, j: (0, z_blk)),
                    pl.BlockSpec((d, d), lambda i, j: (0, g_blk)),
                    pl.BlockSpec((GROUP_COLS, d), lambda i, j: (0, 0)),
                    pl.BlockSpec((d, d), lambda i, j: (0, 0)),
                    pl.BlockSpec((1, 1, d), lambda i, j: (i, 0, 0)),
                    pl.BlockSpec((1, d), lambda i, j: (0, 0))],
        out_specs=tok(d),
        out_shape=jax.ShapeDtypeStruct((b, s, d), F32),
        scratch_shapes=slabs,
        compiler_params=pltpu.CompilerParams(dimension_semantics=("arbitrary", "arbitrary"),
                                             vmem_limit_bytes=VMEM_LIMIT_BYTES),
        name="merge_out",
    )(x, h, gc, *outs, *lses, expand, w_in, w_in, w_att_out, w_o, gate.reshape(b, 1, d),
      final_g.reshape(1, d))


def kernel(x, c, positions, norm_g, w_ada, b_ada, w_in, conv_w, conv_b, conv_ln_g, conv_ln_b,
           w_conv_out, w_att_out, w_o, final_g):
    b, s, d = x.shape
    depth = norm_g.shape[0]
    pos_f = positions.astype(F32)
    dilations = tuple(dl for _, dl in DILATION_GROUPS)
    assert all(window == QBLK * dl for window, dl in DILATION_GROUPS)
    for layer in range(depth):
        mod = _modulation(c, w_ada[layer], b_ada[layer])
        shift, scale, gate = mod[:, :d], mod[:, d:2 * d], mod[:, 2 * d:]
        w_in_b = w_in[layer].astype(BF16)
        h, *h_perm = _modulated_norm(x, norm_g[layer], scale, shift, dilations, ts=1024)
        gc = _conv_branch(h, w_in_b, conv_w[layer], conv_b[layer], conv_ln_g[layer],
                          conv_ln_b[layer], w_conv_out[layer].astype(BF16), tm=1024)
        outs, lses = [], []
        for gi, hp in enumerate([h[:, None]] + h_perm):
            o, lse = _attention_group(hp, pos_f, w_in_b, gi, rows_per_step=1024)
            outs.append(o)
            lses.append(lse)
        x = _merge_and_project(x, h, gc, outs, lses, w_in_b, w_att_out[layer].astype(BF16),
                               w_o[layer].astype(BF16), gate, final_g,
                               final_norm=layer == depth - 1, tm=1024)
    return x
```

```python
import functools

import jax
import jax.numpy as jnp
from jax import lax
from jax.experimental import pallas as pl
from jax.experimental.pallas import tpu as pltpu

HEAD_DIM = 64
HEADS_PER_GROUP = 8
DILATION_GROUPS = ((128, 1), (512, 4), (2048, 16))
ROT_DIM = HEAD_DIM // 4
ROPE_THETA = 500000.0
CONV_SIZE = 31
EPS = 1e-6
NEG_INF = -1e30

LANES = 128
SUBLANES = 8
VMEM_LIMIT_BYTES = 56 * 1024 * 1024

GROUP_COLS = HEADS_PER_GROUP * HEAD_DIM
HEADS_PER_VREG = LANES // HEAD_DIM
PAIRS = GROUP_COLS // LANES
QBLK = 128
CONV_HALO = 32
CONV_ROWS = 64
ROW_SPLIT = 4
ATT_SPLIT = 2

F32 = jnp.float32
BF16 = jnp.bfloat16


def _dot(a, b):
    return jnp.dot(a, b, preferred_element_type=F32)


def _dot_nt(a, b):
    return lax.dot_general(a, b, (((1,), (1,)), ((), ())), preferred_element_type=F32)


def _sigmoid(x):
    return 1.0 / (1.0 + jnp.exp(-x))


def _silu(x):
    return x * _sigmoid(x)


def _emit_skewed(stages, n_blocks):
    state = [dict() for _ in range(n_blocks)]
    for tick in range(len(stages) + n_blocks - 1):
        for blk in range(n_blocks):
            if 0 <= tick - blk < len(stages):
                stages[tick - blk](blk, state[blk])


def _mod_kernel(c_ref, w_ref, b_ref, o_ref):
    o_ref[...] = jnp.dot(c_ref[...], w_ref[...], preferred_element_type=F32,
                         precision=lax.Precision.HIGHEST) + b_ref[...]


def _modulation(c, w_ada, b_ada):
    b, d = c.shape
    n = w_ada.shape[1]
    return pl.pallas_call(
        _mod_kernel,
        grid=(n // d,),
        in_specs=[pl.BlockSpec((b, d), lambda j: (0, 0)),
                  pl.BlockSpec((d, d), lambda j: (0, j)),
                  pl.BlockSpec((1, d), lambda j: (0, j))],
        out_specs=pl.BlockSpec((b, d), lambda j: (0, j)),
        out_shape=jax.ShapeDtypeStruct((b, n), F32),
        compiler_params=pltpu.CompilerParams(dimension_semantics=("arbitrary",),
                                             vmem_limit_bytes=VMEM_LIMIT_BYTES),
        name="mod",
    )(c, w_ada, b_ada.reshape(1, n))


def _h_kernel(x_ref, g_ref, scale_ref, shift_ref, h_ref, *rest, ratios):
    perm_refs, slabs = rest[:len(ratios)], rest[len(ratios):]
    ts, d = x_ref.shape[1], x_ref.shape[2]
    ncb = d // LANES
    x = x_ref[0]
    y = x * lax.rsqrt(jnp.mean(x * x, axis=-1, keepdims=True) + EPS)
    hf = (y * g_ref[...]) * (1.0 + scale_ref[0]) + shift_ref[0]
    h_ref[0] = hf.astype(BF16)
    for cb in range(ncb):
        slabs[0][0, cb] = hf[:, cb * LANES:(cb + 1) * LANES]
    n_prev, rows = 1, ts
    for lvl, ratio in enumerate(ratios):
        src, out_ref = slabs[lvl], perm_refs[lvl]
        dst = slabs[lvl + 1] if lvl + 1 < len(ratios) else None
        for rp in range(n_prev):
            for q in range(ratio):
                r = q * n_prev + rp
                for cb in range(ncb):
                    part = src[rp, cb, pl.ds(q, rows // ratio, stride=ratio), :]
                    if dst is not None:
                        dst[r, cb] = part
                    out_ref[0, r, :, cb * LANES:(cb + 1) * LANES] = part.astype(BF16)
        n_prev, rows = n_prev * ratio, rows // ratio


def _modulated_norm(x, g, scale, shift, dilations, ts):
    b, s, d = x.shape
    ratios = tuple(dn // dp for dp, dn in zip(dilations[:-1], dilations[1:]))
    assert dilations[0] == 1 and all(dp * r == dn for dp, dn, r in zip(dilations, dilations[1:], ratios))
    ncb = d // LANES
    vec = pl.BlockSpec((1, 1, d), lambda i, j: (i, 0, 0))
    perm_shapes = [jax.ShapeDtypeStruct((b, dl, s // dl, d), BF16) for dl in dilations[1:]]
    perm_specs = [pl.BlockSpec((1, dl, ts // dl, d), lambda i, j: (i, 0, j, 0)) for dl in dilations[1:]]
    slabs = [pltpu.VMEM((dl, ncb, ts // dl, LANES), F32) for dl in dilations[:-1]]
    return pl.pallas_call(
        functools.partial(_h_kernel, ratios=ratios),
        grid=(b, s // ts),
        in_specs=[pl.BlockSpec((1, ts, d), lambda i, j: (i, j, 0)),
                  pl.BlockSpec((1, d), lambda i, j: (0, 0)), vec, vec],
        out_specs=[pl.BlockSpec((1, ts, d), lambda i, j: (i, j, 0))] + perm_specs,
        out_shape=[jax.ShapeDtypeStruct((b, s, d), BF16)] + perm_shapes,
        scratch_shapes=slabs,
        compiler_params=pltpu.CompilerParams(dimension_semantics=("arbitrary", "arbitrary"),
                                             vmem_limit_bytes=VMEM_LIMIT_BYTES),
        name="hnorm",
    )(x, g.reshape(1, d), scale.reshape(b, 1, d), shift.reshape(b, 1, d))


def _conv_kernel(h_ref, wa_ref, wb_ref, wz_ref, wg_ref, cw_ref, cb_ref, lg_ref, lb_ref, wo_ref,
                 out_ref, ubuf, cbuf):
    tm = h_ref.shape[1]
    ncb = ubuf.shape[0]

    @pl.when(pl.program_id(1) == 0)
    def _():
        ubuf[:, 0:CONV_HALO, :] = jnp.zeros((ncb, CONV_HALO, LANES), F32)

    @pl.when(pl.program_id(1) > 0)
    def _():
        ubuf[:, 0:CONV_HALO, :] = ubuf[:, tm:tm + CONV_HALO, :]

    sub = tm // ROW_SPLIT

    def rows_of(sb):
        return slice(sb * sub, (sb + 1) * sub)

    def stage_glu_dots(sb, st):
        hs = h_ref[0, rows_of(sb), :]
        st["a"] = _dot(hs, wa_ref[...])
        st["b"] = _dot(hs, wb_ref[...])

    def stage_glu(sb, st):
        u = st.pop("a") * _sigmoid(st.pop("b"))
        for cb in range(ncb):
            ubuf[cb, CONV_HALO + sb * sub:CONV_HALO + (sb + 1) * sub, :] = u[:, cb * LANES:(cb + 1) * LANES]

    _emit_skewed((stage_glu_dots, stage_glu), ROW_SPLIT)

    first_tap = CONV_HALO - (CONV_SIZE - 1)

    def row_block(rb, carry):
        r0 = pl.multiple_of(rb * CONV_ROWS, CONV_ROWS)
        for cb in range(ncb):
            cols = slice(cb * LANES, (cb + 1) * LANES)
            acc = jnp.broadcast_to(cb_ref[:, cols], (CONV_ROWS, LANES))
            for t in range(CONV_SIZE):
                acc = acc + ubuf[cb, pl.ds(r0 + first_tap + t, CONV_ROWS), :] * cw_ref[t:t + 1, cols]
            cbuf[pl.ds(r0, CONV_ROWS), cols] = acc
        return carry

    lax.fori_loop(0, tm // CONV_ROWS, row_block, 0)

    def stage_z(sb, st):
        st["z"] = _dot(h_ref[0, rows_of(sb), :], wz_ref[...])

    def stage_norm(sb, st):
        c = cbuf[rows_of(sb), :]
        mu = jnp.mean(c, axis=-1, keepdims=True)
        dlt = c - mu
        var = jnp.mean(dlt * dlt, axis=-1, keepdims=True)
        st["y"] = _silu(dlt * lax.rsqrt(var + EPS) * lg_ref[...] + lb_ref[...])

    def stage_gate(sb, st):
        st["y"] = (st.pop("y") * _silu(st.pop("z"))).astype(BF16)

    def stage_out(sb, st):
        st["yc"] = _dot(st.pop("y"), wo_ref[...])

    def stage_g(sb, st):
        st["g"] = _dot(h_ref[0, rows_of(sb), :], wg_ref[...])

    def stage_store(sb, st):
        out_ref[0, rows_of(sb), :] = (_sigmoid(st.pop("g")) * st.pop("yc")).astype(BF16)

    _emit_skewed((stage_z, stage_norm, stage_gate, stage_out, stage_g, stage_store), ROW_SPLIT)


def _conv_branch(h, w_in, conv_w, conv_b, ln_g, ln_b, w_out, tm):
    b, s, d = h.shape
    cw = conv_w.shape[1]
    wcol = lambda blk: pl.BlockSpec((d, cw), lambda i, j: (0, blk))
    row = pl.BlockSpec((1, cw), lambda i, j: (0, 0))
    return pl.pallas_call(
        _conv_kernel,
        grid=(b, s // tm),
        in_specs=[pl.BlockSpec((1, tm, d), lambda i, j: (i, j, 0)),
                  wcol(0), wcol(1), wcol(2), wcol(8),
                  pl.BlockSpec((CONV_SIZE, cw), lambda i, j: (0, 0)), row, row, row,
                  pl.BlockSpec((cw, d), lambda i, j: (0, 0))],
        out_specs=pl.BlockSpec((1, tm, d), lambda i, j: (i, j, 0)),
        out_shape=jax.ShapeDtypeStruct((b, s, d), BF16),
        scratch_shapes=[pltpu.VMEM((cw // LANES, CONV_HALO + tm, LANES), F32),
                        pltpu.VMEM((tm, cw), F32)],
        compiler_params=pltpu.CompilerParams(dimension_semantics=("arbitrary", "arbitrary"),
                                             vmem_limit_bytes=VMEM_LIMIT_BYTES),
        name="conv_branch",
    )(h, w_in, w_in, w_in, w_in, conv_w, conv_b.reshape(1, cw), ln_g.reshape(1, cw),
      ln_b.reshape(1, cw), w_out)


def _att_kernel(h_ref, pos_ref, invf_ref, wq_ref, wk_ref, wv_ref, o_ref, lse_ref,
                qbuf, kbuf, vbuf):
    n_sub, tq, d = h_ref.shape[1], h_ref.shape[2], h_ref.shape[3]
    rows_all = n_sub * tq
    part = rows_all // ATT_SPLIT
    n = pl.program_id(2)

    @pl.when(n == 0)
    def _():
        kbuf[0:QBLK, :] = jnp.zeros((QBLK, GROUP_COLS), BF16)
        vbuf[0:QBLK, :] = jnp.zeros((QBLK, GROUP_COLS), BF16)

    @pl.when(n > 0)
    def _():
        kbuf[0:QBLK, :] = kbuf[rows_all:rows_all + QBLK, :]
        vbuf[0:QBLK, :] = vbuf[rows_all:rows_all + QBLK, :]

    lane_t = lax.broadcasted_iota(jnp.int32, (part, LANES), 1)
    low_half = (lane_t & (HEAD_DIM - 1)) < (ROT_DIM // 2)
    head0_t = lane_t < HEAD_DIM
    row = lax.broadcasted_iota(jnp.int32, (QBLK, 2 * QBLK), 0)
    col = lax.broadcasted_iota(jnp.int32, (QBLK, 2 * QBLK), 1)
    band = ((col < QBLK) & (col >= row)) | ((col >= QBLK) & (col - QBLK <= row))
    band_new_sub = band & (col >= QBLK)
    band_first = band & ((col >= QBLK) | (n > 0))
    lane = lax.broadcasted_iota(jnp.int32, (QBLK, LANES), 1)
    blocks_per_sub = tq // QBLK

    def stage_project(blk, st):
        hs = h_ref[0].reshape(rows_all, d)[blk * part:(blk + 1) * part, :]
        st["q"] = _dot(hs, wq_ref[...])
        st["k"] = _dot(hs, wk_ref[...])
        vbuf[QBLK + blk * part:QBLK + (blk + 1) * part, :] = _dot(hs, wv_ref[...]).astype(BF16)

    def stage_rotary(blk, st):
        pos = jnp.concatenate([pos_ref[0, r] for r in range(n_sub)], axis=1)[:, blk * part:(blk + 1) * part]
        ang = invf_ref[...] * pos
        cos_a, sin_a = jnp.cos(ang), jnp.sin(ang)
        rest = (HEAD_DIM - ROT_DIM, part)
        cs = jnp.concatenate([cos_a, cos_a, jnp.ones(rest, F32)] * HEADS_PER_VREG, axis=0).T
        sn = jnp.concatenate([-sin_a, sin_a, jnp.zeros(rest, F32)] * HEADS_PER_VREG, axis=0).T

        def rope(t):
            partner = jnp.where(low_half, pltpu.roll(t, LANES - ROT_DIM // 2, 1),
                                pltpu.roll(t, ROT_DIM // 2, 1))
            return t * cs + partner * sn

        q, k = st.pop("q"), st.pop("k")
        rows = slice(blk * part, (blk + 1) * part)
        for p in range(PAIRS):
            cols = slice(p * LANES, (p + 1) * LANES)
            qr = rope(q[:, cols]) * (HEAD_DIM ** -0.5)
            qbuf[0, rows, cols] = jnp.where(head0_t, qr, 0.0).astype(BF16)
            qbuf[1, rows, cols] = jnp.where(head0_t, 0.0, qr).astype(BF16)
            kbuf[QBLK + blk * part:QBLK + (blk + 1) * part, cols] = rope(k[:, cols]).astype(BF16)

    def stage_attend(blk, st):
        for qb in range(blk * part // QBLK, (blk + 1) * part // QBLK):
            r, qs = qb // blocks_per_sub, qb % blocks_per_sub
            rows = slice(qb * QBLK, (qb + 1) * QBLK)
            win = slice(qb * QBLK, (qb + 2) * QBLK)
            sub_rows = slice(qs * QBLK, (qs + 1) * QBLK)
            valid = band if qs else (band_new_sub if r else band_first)
            stats = jnp.zeros((QBLK, LANES), F32)
            for p in range(PAIRS):
                cols = slice(p * LANES, (p + 1) * LANES)
                kw = kbuf[win, cols]
                vw = vbuf[win, cols]
                outs = []
                for hh in range(HEADS_PER_VREG):
                    s = jnp.where(valid, _dot_nt(qbuf[hh, rows, cols], kw), NEG_INF)
                    m = jnp.max(s, axis=-1, keepdims=True)
                    e = jnp.exp(s - m)
                    den = jnp.sum(e, axis=-1, keepdims=True)
                    outs.append(_dot(e.astype(BF16), vw))
                    head = p * HEADS_PER_VREG + hh
                    stats = jnp.where(lane == head, m, jnp.where(lane == HEADS_PER_GROUP + head, den, stats))
                o_ref[0, r, sub_rows, cols] = jnp.where(lane < HEAD_DIM, outs[0], outs[1]).astype(BF16)
            lse_ref[0, r, sub_rows, :] = stats

    _emit_skewed((stage_project, stage_rotary, stage_attend), ATT_SPLIT)


def _attention_group(hp, pos_f, w_in, gi, rows_per_step):
    b, dilation, sub_len, d = hp.shape
    tq = min(rows_per_step, sub_len)
    n_sub = min(rows_per_step // tq, dilation)
    assert n_sub == 1 or tq == sub_len
    pos = pos_f.reshape(b, sub_len, dilation).transpose(0, 2, 1)[:, :, None, :]
    half = ROT_DIM // 2
    invf = (ROPE_THETA ** (-(jnp.arange(half, dtype=F32) * 2.0 / ROT_DIM))).reshape(half, 1)
    col0 = (d * 3) // GROUP_COLS
    n_grp = len(DILATION_GROUPS)
    wspec = lambda blk: pl.BlockSpec((d, GROUP_COLS), lambda i, r, n: (0, blk))
    tile = lambda width: pl.BlockSpec((1, n_sub, tq, width), lambda i, r, n: (i, r, n, 0))
    return pl.pallas_call(
        _att_kernel,
        grid=(b, dilation // n_sub, sub_len // tq),
        in_specs=[tile(d),
                  pl.BlockSpec((1, n_sub, 1, tq), lambda i, r, n: (i, r, 0, n)),
                  pl.BlockSpec((half, 1), lambda i, r, n: (0, 0)),
                  wspec(col0 + gi), wspec(col0 + n_grp + gi), wspec(col0 + 2 * n_grp + gi)],
        out_specs=[tile(GROUP_COLS), tile(LANES)],
        out_shape=[jax.ShapeDtypeStruct((b, dilation, sub_len, GROUP_COLS), BF16),
                   jax.ShapeDtypeStruct((b, dilation, sub_len, LANES), F32)],
        scratch_shapes=[pltpu.VMEM((HEADS_PER_VREG, n_sub * tq, GROUP_COLS), BF16),
                        pltpu.VMEM((QBLK + n_sub * tq, GROUP_COLS), BF16),
                        pltpu.VMEM((QBLK + n_sub * tq, GROUP_COLS), BF16)],
        compiler_params=pltpu.CompilerParams(
            dimension_semantics=("arbitrary", "arbitrary", "arbitrary"),
            vmem_limit_bytes=VMEM_LIMIT_BYTES),
        name=f"att_d{dilation}",
    )(hp, pos, invf, w_in, w_in, w_in)


def _final_kernel(x_ref, h_ref, gc_ref, *rest, dilations, final_norm):
    ng = len(dilations)
    o_refs, l_refs = rest[:ng], rest[ng:2 * ng]
    ex_ref, wz_ref, wg_ref, wao_ref, wo_ref, gate_ref, fg_ref, out_ref = rest[2 * ng:2 * ng + 8]
    slabs = rest[2 * ng + 8:]
    tm = x_ref.shape[1]

    for gi, dl in enumerate(dilations):
        if dl == 1:
            continue
        o_slab, l_slab = slabs[2 * (gi - 1)], slabs[2 * (gi - 1) + 1]
        for r in range(dl):
            rows = pl.ds(r, tm // dl, stride=dl)
            l_slab[rows, :] = l_refs[gi][0, r]
            for p in range(PAIRS):
                o_slab[p, rows, :] = o_refs[gi][0, r, :, p * LANES:(p + 1) * LANES].astype(F32)

    sub = tm // ROW_SPLIT
    head_lane = lax.broadcasted_iota(jnp.int32, (sub, LANES), 1) < HEADS_PER_GROUP

    def rows_of(sb):
        return slice(sb * sub, (sb + 1) * sub)

    def stage_gates(sb, st):
        h = h_ref[0, rows_of(sb), :]
        st["z"] = _dot(h, wz_ref[...])
        st["g"] = _dot(h, wg_ref[...])

    def stage_merge(sb, st):
        rows = rows_of(sb)
        lses, o_nat = [], []
        for gi, dl in enumerate(dilations):
            if dl == 1:
                lses.append(l_refs[gi][0, 0, rows, :])
                o_nat.append([o_refs[gi][0, 0, rows, p * LANES:(p + 1) * LANES].astype(F32)
                              for p in range(PAIRS)])
            else:
                o_slab, l_slab = slabs[2 * (gi - 1)], slabs[2 * (gi - 1) + 1]
                lses.append(l_slab[rows, :])
                o_nat.append([o_slab[p, rows, :] for p in range(PAIRS)])
        m = functools.reduce(jnp.maximum, lses)
        es = [jnp.exp(l - m) for l in lses]
        dens = [pltpu.roll(l, LANES - HEADS_PER_GROUP, 1) for l in lses]
        z = functools.reduce(lambda a, c: a + c, [e * dn for e, dn in zip(es, dens)])
        inv = 1.0 / jnp.where(head_lane, z, 1.0)
        wfs = []
        for e in es:
            w = jnp.where(head_lane, e * inv, 0.0)
            wfs.append(_dot(w.astype(BF16), ex_ref[...]))
        att = jnp.concatenate(
            [functools.reduce(lambda a, c: a + c,
                              [wf[:, p * LANES:(p + 1) * LANES] * o[p] for wf, o in zip(wfs, o_nat)])
             for p in range(PAIRS)], axis=1)
        st["a"] = (att * _silu(st.pop("z"))).astype(BF16)

    def stage_att_out(sb, st):
        st["ya"] = _dot(st.pop("a"), wao_ref[...])

    def stage_gate_merge(sb, st):
        merged = gc_ref[0, rows_of(sb), :].astype(F32) + _sigmoid(st.pop("g")) * st.pop("ya")
        st["merged"] = merged.astype(BF16)

    def stage_w_o(sb, st):
        st["r"] = _dot(st.pop("merged"), wo_ref[...])

    def stage_residual(sb, st):
        xn = x_ref[0, rows_of(sb), :] + gate_ref[0] * st.pop("r")
        if final_norm:
            xn = xn * lax.rsqrt(jnp.mean(xn * xn, axis=-1, keepdims=True) + EPS) * fg_ref[...]
        out_ref[0, rows_of(sb), :] = xn

    _emit_skewed((stage_gates, stage_merge, stage_att_out, stage_gate_merge, stage_w_o, stage_residual),
                 ROW_SPLIT)


def _merge_and_project(x, h, gc, outs, lses, w_in, w_att_out, w_o, gate, final_g, final_norm, tm):
    b, s, d = x.shape
    dilations = tuple(o.shape[1] for o in outs)
    assert dilations[0] == 1
    tok = lambda width: pl.BlockSpec((1, tm, width), lambda i, j: (i, j, 0))
    perm = lambda dl, width: pl.BlockSpec((1, dl, tm // dl, width), lambda i, j: (i, 0, j, 0))
    head = jnp.arange(LANES)[:, None]
    expand = (head == jnp.arange(GROUP_COLS)[None, :] // HEAD_DIM).astype(BF16)
    z_blk = w_in.shape[1] // GROUP_COLS - 2 * (d // GROUP_COLS) - 1
    g_blk = w_in.shape[1] // d - 1
    slabs = []
    for dl in dilations[1:]:
        slabs += [pltpu.VMEM((PAIRS, tm, LANES), F32), pltpu.VMEM((tm, LANES), F32)]
    return pl.pallas_call(
        functools.partial(_final_kernel, dilations=dilations, final_norm=final_norm),
        grid=(b, s // tm),
        in_specs=[tok(d), tok(d), tok(d)]
                 + [perm(dl, GROUP_COLS) for dl in dilations] + [perm(dl, LANES) for dl in dilations]
                 + [pl.BlockSpec((LANES, GROUP_COLS), lambda i, j: (0, 0)),
                    pl.BlockSpec((d, GROUP_COLS), lambda i, j: (0, z_blk)),
                    pl.BlockSpec((d, d), lambda i, j: (0, g_blk)),
                    pl.BlockSpec((GROUP_COLS, d), lambda i, j: (0, 0)),
                    pl.BlockSpec((d, d), lambda i, j: (0, 0)),
                    pl.BlockSpec((1, 1, d), lambda i, j: (i, 0, 0)),
                    pl.BlockSpec((1, d), lambda i, j: (0, 0))],
        out_specs=tok(d),
        out_shape=jax.ShapeDtypeStruct((b, s, d), F32),
        scratch_shapes=slabs,
        compiler_params=pltpu.CompilerParams(dimension_semantics=("arbitrary", "arbitrary"),
                                             vmem_limit_bytes=VMEM_LIMIT_BYTES),
        name="merge_out",
    )(x, h, gc, *outs, *lses, expand, w_in, w_in, w_att_out, w_o, gate.reshape(b, 1, d),
      final_g.reshape(1, d))


def kernel(x, c, positions, norm_g, w_ada, b_ada, w_in, conv_w, conv_b, conv_ln_g, conv_ln_b,
           w_conv_out, w_att_out, w_o, final_g):
    b, s, d = x.shape
    depth = norm_g.shape[0]
    pos_f = positions.astype(F32)
    dilations = tuple(dl for _, dl in DILATION_GROUPS)
    assert all(window == QBLK * dl for window, dl in DILATION_GROUPS)
    for layer in range(depth):
        mod = _modulation(c, w_ada[layer], b_ada[layer])
        shift, scale, gate = mod[:, :d], mod[:, d:2 * d], mod[:, 2 * d:]
        w_in_b = w_in[layer].astype(BF16)
        h, *h_perm = _modulated_norm(x, norm_g[layer], scale, shift, dilations, ts=1024)
        gc = _conv_branch(h, w_in_b, conv_w[layer], conv_b[layer], conv_ln_g[layer],
                          conv_ln_b[layer], w_conv_out[layer].astype(BF16), tm=1024)
        outs, lses = [], []
        for gi, hp in enumerate([h[:, None]] + h_perm):
            o, lse = _attention_group(hp, pos_f, w_in_b, gi, rows_per_step=1024)
            outs.append(o)
            lses.append(lse)
        x = _merge_and_project(x, h, gc, outs, lses, w_in_b, w_att_out[layer].astype(BF16),
                               w_o[layer].astype(BF16), gate, final_g,
                               final_norm=layer == depth - 1, tm=1024)
    return x
```

```python
import functools

import jax
import jax.numpy as jnp
from jax import lax
from jax.experimental import pallas as pl
from jax.experimental.pallas import tpu as pltpu

HEAD_DIM = 64
HEADS_PER_GROUP = 8
DILATION_GROUPS = ((128, 1), (512, 4), (2048, 16))
ROT_DIM = HEAD_DIM // 4
ROPE_THETA = 500000.0
CONV_SIZE = 31
EPS = 1e-6
NEG_INF = -1e30

LANES = 128
SUBLANES = 8
VMEM_LIMIT_BYTES = 56 * 1024 * 1024

GROUP_COLS = HEADS_PER_GROUP * HEAD_DIM
HEADS_PER_VREG = LANES // HEAD_DIM
PAIRS = GROUP_COLS // LANES
QBLK = 128
CONV_HALO = 32
CONV_ROWS = 64
NORM_ROWS = 128
ROW_SPLIT = 4
ATT_SPLIT = 2

F32 = jnp.float32
BF16 = jnp.bfloat16


def _dot(a, b):
    return jnp.dot(a, b, preferred_element_type=F32)


def _dot_nt(a, b):
    return lax.dot_general(a, b, (((1,), (1,)), ((), ())), preferred_element_type=F32)


def _sigmoid(x):
    return 1.0 / (1.0 + jnp.exp(-x))


def _silu(x):
    return x * _sigmoid(x)


def _emit_skewed(stages, n_blocks):
    state = [dict() for _ in range(n_blocks)]
    for tick in range(len(stages) + n_blocks - 1):
        for blk in range(n_blocks):
            if 0 <= tick - blk < len(stages):
                stages[tick - blk](blk, state[blk])


def _mod_kernel(c_ref, w_ref, b_ref, o_ref):
    o_ref[...] = jnp.dot(c_ref[...], w_ref[...], preferred_element_type=F32,
                         precision=lax.Precision.HIGHEST) + b_ref[...]


def _modulation(c, w_ada, b_ada):
    b, d = c.shape
    n = w_ada.shape[1]
    return pl.pallas_call(
        _mod_kernel,
        grid=(n // d,),
        in_specs=[pl.BlockSpec((b, d), lambda j: (0, 0)),
                  pl.BlockSpec((d, d), lambda j: (0, j)),
                  pl.BlockSpec((1, d), lambda j: (0, j))],
        out_specs=pl.BlockSpec((b, d), lambda j: (0, j)),
        out_shape=jax.ShapeDtypeStruct((b, n), F32),
        compiler_params=pltpu.CompilerParams(dimension_semantics=("arbitrary",),
                                             vmem_limit_bytes=VMEM_LIMIT_BYTES),
        name="mod",
    )(c, w_ada, b_ada.reshape(1, n))


def _h_kernel(x_ref, g_ref, scale_ref, shift_ref, h_ref, *rest, ratios):
    perm_refs, slabs = rest[:len(ratios)], rest[len(ratios):]
    ts, d = x_ref.shape[1], x_ref.shape[2]
    ncb = d // LANES
    gain = g_ref[...] * (1.0 + scale_ref[0])
    for rb in range(ts // NORM_ROWS):
        rows = slice(rb * NORM_ROWS, (rb + 1) * NORM_ROWS)
        x = x_ref[0, rows, :]
        y = x * lax.rsqrt(jnp.mean(x * x, axis=-1, keepdims=True) + EPS)
        hf = y * gain + shift_ref[0]
        h_ref[0, rows, :] = hf.astype(BF16)
        for cb in range(ncb):
            slabs[0][0, cb, rows, :] = hf[:, cb * LANES:(cb + 1) * LANES]
    n_prev, rows = 1, ts
    for lvl, ratio in enumerate(ratios):
        src, out_ref = slabs[lvl], perm_refs[lvl]
        dst = slabs[lvl + 1] if lvl + 1 < len(ratios) else None
        for rp in range(n_prev):
            for q in range(ratio):
                r = q * n_prev + rp
                for cb in range(ncb):
                    part = src[rp, cb, pl.ds(q, rows // ratio, stride=ratio), :]
                    if dst is not None:
                        dst[r, cb] = part
                    out_ref[0, r, :, cb * LANES:(cb + 1) * LANES] = part.astype(BF16)
        n_prev, rows = n_prev * ratio, rows // ratio


def _modulated_norm(x, g, scale, shift, dilations, ts):
    b, s, d = x.shape
    ratios = tuple(dn // dp for dp, dn in zip(dilations[:-1], dilations[1:]))
    assert dilations[0] == 1 and all(dp * r == dn for dp, dn, r in zip(dilations, dilations[1:], ratios))
    ncb = d // LANES
    vec = pl.BlockSpec((1, 1, d), lambda i, j: (i, 0, 0))
    perm_shapes = [jax.ShapeDtypeStruct((b, dl, s // dl, d), BF16) for dl in dilations[1:]]
    perm_specs = [pl.BlockSpec((1, dl, ts // dl, d), lambda i, j: (i, 0, j, 0)) for dl in dilations[1:]]
    slabs = [pltpu.VMEM((dl, ncb, ts // dl, LANES), F32) for dl in dilations[:-1]]
    return pl.pallas_call(
        functools.partial(_h_kernel, ratios=ratios),
        grid=(b, s // ts),
        in_specs=[pl.BlockSpec((1, ts, d), lambda i, j: (i, j, 0)),
                  pl.BlockSpec((1, d), lambda i, j: (0, 0)), vec, vec],
        out_specs=[pl.BlockSpec((1, ts, d), lambda i, j: (i, j, 0))] + perm_specs,
        out_shape=[jax.ShapeDtypeStruct((b, s, d), BF16)] + perm_shapes,
        scratch_shapes=slabs,
        compiler_params=pltpu.CompilerParams(dimension_semantics=("arbitrary", "arbitrary"),
                                             vmem_limit_bytes=VMEM_LIMIT_BYTES),
        name="hnorm",
    )(x, g.reshape(1, d), scale.reshape(b, 1, d), shift.reshape(b, 1, d))


def _conv_kernel(h_ref, wa_ref, wb_ref, wz_ref, wg_ref, cw_ref, cb_ref, lg_ref, lb_ref, wo_ref,
                 out_ref, ubuf, cbuf):
    tm = h_ref.shape[1]
    ncb = ubuf.shape[0]

    @pl.when(pl.program_id(1) == 0)
    def _():
        ubuf[:, 0:CONV_HALO, :] = jnp.zeros((ncb, CONV_HALO, LANES), F32)

    @pl.when(pl.program_id(1) > 0)
    def _():
        ubuf[:, 0:CONV_HALO, :] = ubuf[:, tm:tm + CONV_HALO, :]

    sub = tm // ROW_SPLIT

    def rows_of(sb):
        return slice(sb * sub, (sb + 1) * sub)

    def stage_glu_dots(sb, st):
        hs = h_ref[0, rows_of(sb), :]
        st["a"] = _dot(hs, wa_ref[...])
        st["b"] = _dot(hs, wb_ref[...])

    def stage_glu(sb, st):
        u = st.pop("a") * _sigmoid(st.pop("b"))
        for cb in range(ncb):
            ubuf[cb, CONV_HALO + sb * sub:CONV_HALO + (sb + 1) * sub, :] = u[:, cb * LANES:(cb + 1) * LANES]

    _emit_skewed((stage_glu_dots, stage_glu), ROW_SPLIT)

    first_tap = CONV_HALO - (CONV_SIZE - 1)

    def row_block(rb, carry):
        r0 = pl.multiple_of(rb * CONV_ROWS, CONV_ROWS)
        for cb in range(ncb):
            cols = slice(cb * LANES, (cb + 1) * LANES)
            acc = jnp.broadcast_to(cb_ref[:, cols], (CONV_ROWS, LANES))
            for t in range(CONV_SIZE):
                acc = acc + ubuf[cb, pl.ds(r0 + first_tap + t, CONV_ROWS), :] * cw_ref[t:t + 1, cols]
            cbuf[pl.ds(r0, CONV_ROWS), cols] = acc
        return carry

    lax.fori_loop(0, tm // CONV_ROWS, row_block, 0)

    def stage_z(sb, st):
        st["z"] = _dot(h_ref[0, rows_of(sb), :], wz_ref[...])

    def stage_norm(sb, st):
        c = cbuf[rows_of(sb), :]
        mu = jnp.mean(c, axis=-1, keepdims=True)
        dlt = c - mu
        var = jnp.mean(dlt * dlt, axis=-1, keepdims=True)
        st["y"] = _silu(dlt * lax.rsqrt(var + EPS) * lg_ref[...] + lb_ref[...])

    def stage_gate(sb, st):
        st["y"] = (st.pop("y") * _silu(st.pop("z"))).astype(BF16)

    def stage_out(sb, st):
        st["yc"] = _dot(st.pop("y"), wo_ref[...])

    def stage_g(sb, st):
        st["g"] = _dot(h_ref[0, rows_of(sb), :], wg_ref[...])

    def stage_store(sb, st):
        out_ref[0, rows_of(sb), :] = (_sigmoid(st.pop("g")) * st.pop("yc")).astype(BF16)

    _emit_skewed((stage_z, stage_norm, stage_gate, stage_out, stage_g, stage_store), ROW_SPLIT)


def _conv_branch(h, w_in, conv_w, conv_b, ln_g, ln_b, w_out, tm):
    b, s, d = h.shape
    cw = conv_w.shape[1]
    wcol = lambda blk: pl.BlockSpec((d, cw), lambda i, j: (0, blk))
    row = pl.BlockSpec((1, cw), lambda i, j: (0, 0))
    return pl.pallas_call(
        _conv_kernel,
        grid=(b, s // tm),
        in_specs=[pl.BlockSpec((1, tm, d), lambda i, j: (i, j, 0)),
                  wcol(0), wcol(1), wcol(2), wcol(8),
                  pl.BlockSpec((CONV_SIZE, cw), lambda i, j: (0, 0)), row, row, row,
                  pl.BlockSpec((cw, d), lambda i, j: (0, 0))],
        out_specs=pl.BlockSpec((1, tm, d), lambda i, j: (i, j, 0)),
        out_shape=jax.ShapeDtypeStruct((b, s, d), BF16),
        scratch_shapes=[pltpu.VMEM((cw // LANES, CONV_HALO + tm, LANES), F32),
                        pltpu.VMEM((tm, cw), F32)],
        compiler_params=pltpu.CompilerParams(dimension_semantics=("arbitrary", "arbitrary"),
                                             vmem_limit_bytes=VMEM_LIMIT_BYTES),
        name="conv_branch",
    )(h, w_in, w_in, w_in, w_in, conv_w, conv_b.reshape(1, cw), ln_g.reshape(1, cw),
      ln_b.reshape(1, cw), w_out)


def _att_kernel(h_ref, pos_ref, invf_ref, wq_ref, wk_ref, wv_ref, o_ref, lse_ref,
                qbuf, kbuf, vbuf):
    n_sub, tq, d = h_ref.shape[1], h_ref.shape[2], h_ref.shape[3]
    rows_all = n_sub * tq
    part = rows_all // ATT_SPLIT
    n = pl.program_id(2)

    @pl.when(n == 0)
    def _():
        kbuf[0:QBLK, :] = jnp.zeros((QBLK, GROUP_COLS), BF16)
        vbuf[0:QBLK, :] = jnp.zeros((QBLK, GROUP_COLS), BF16)

    @pl.when(n > 0)
    def _():
        kbuf[0:QBLK, :] = kbuf[rows_all:rows_all + QBLK, :]
        vbuf[0:QBLK, :] = vbuf[rows_all:rows_all + QBLK, :]

    lane_t = lax.broadcasted_iota(jnp.int32, (part, LANES), 1)
    low_half = (lane_t & (HEAD_DIM - 1)) < (ROT_DIM // 2)
    head0_t = lane_t < HEAD_DIM
    row = lax.broadcasted_iota(jnp.int32, (QBLK, 2 * QBLK), 0)
    col = lax.broadcasted_iota(jnp.int32, (QBLK, 2 * QBLK), 1)
    band = ((col < QBLK) & (col >= row)) | ((col >= QBLK) & (col - QBLK <= row))
    band_new_sub = band & (col >= QBLK)
    band_first = band & ((col >= QBLK) | (n > 0))
    lane = lax.broadcasted_iota(jnp.int32, (QBLK, LANES), 1)
    blocks_per_sub = tq // QBLK

    def stage_project(blk, st):
        hs = h_ref[0].reshape(rows_all, d)[blk * part:(blk + 1) * part, :]
        st["q"] = _dot(hs, wq_ref[...])
        st["k"] = _dot(hs, wk_ref[...])
        vbuf[QBLK + blk * part:QBLK + (blk + 1) * part, :] = _dot(hs, wv_ref[...]).astype(BF16)

    def stage_rotary(blk, st):
        pos = jnp.concatenate([pos_ref[0, r] for r in range(n_sub)], axis=1)[:, blk * part:(blk + 1) * part]
        ang = invf_ref[...] * pos
        cos_a, sin_a = jnp.cos(ang), jnp.sin(ang)
        rest = (HEAD_DIM - ROT_DIM, part)
        cs = jnp.concatenate([cos_a, cos_a, jnp.ones(rest, F32)] * HEADS_PER_VREG, axis=0).T
        sn = jnp.concatenate([-sin_a, sin_a, jnp.zeros(rest, F32)] * HEADS_PER_VREG, axis=0).T

        def rope(t):
            partner = jnp.where(low_half, pltpu.roll(t, LANES - ROT_DIM // 2, 1),
                                pltpu.roll(t, ROT_DIM // 2, 1))
            return t * cs + partner * sn

        q, k = st.pop("q"), st.pop("k")
        rows = slice(blk * part, (blk + 1) * part)
        for p in range(PAIRS):
            cols = slice(p * LANES, (p + 1) * LANES)
            qr = rope(q[:, cols]) * (HEAD_DIM ** -0.5)
            qbuf[0, rows, cols] = jnp.where(head0_t, qr, 0.0).astype(BF16)
            qbuf[1, rows, cols] = jnp.where(head0_t, 0.0, qr).astype(BF16)
            kbuf[QBLK + blk * part:QBLK + (blk + 1) * part, cols] = rope(k[:, cols]).astype(BF16)

    def stage_attend(blk, st):
        for qb in range(blk * part // QBLK, (blk + 1) * part // QBLK):
            r, qs = qb // blocks_per_sub, qb % blocks_per_sub
            rows = slice(qb * QBLK, (qb + 1) * QBLK)
            win = slice(qb * QBLK, (qb + 2) * QBLK)
            sub_rows = slice(qs * QBLK, (qs + 1) * QBLK)
            valid = band if qs else (band_new_sub if r else band_first)
            stats = jnp.zeros((QBLK, LANES), F32)
            for p in range(PAIRS):
                cols = slice(p * LANES, (p + 1) * LANES)
                kw = kbuf[win, cols]
                vw = vbuf[win, cols]
                outs = []
                for hh in range(HEADS_PER_VREG):
                    s = jnp.where(valid, _dot_nt(qbuf[hh, rows, cols], kw), NEG_INF)
                    m = jnp.max(s, axis=-1, keepdims=True)
                    e = jnp.exp(s - m)
                    den = jnp.sum(e, axis=-1, keepdims=True)
                    outs.append(_dot(e.astype(BF16), vw))
                    head = p * HEADS_PER_VREG + hh
                    stats = jnp.where(lane == head, m, jnp.where(lane == HEADS_PER_GROUP + head, den, stats))
                o_ref[0, r, sub_rows, cols] = jnp.where(lane < HEAD_DIM, outs[0], outs[1]).astype(BF16)
            lse_ref[0, r, sub_rows, :] = stats

    _emit_skewed((stage_project, stage_rotary, stage_attend), ATT_SPLIT)


def _attention_group(hp, pos_f, w_in, gi, rows_per_step):
    b, dilation, sub_len, d = hp.shape
    tq = min(rows_per_step, sub_len)
    n_sub = min(rows_per_step // tq, dilation)
    assert n_sub == 1 or tq == sub_len
    pos = pos_f.reshape(b, sub_len, dilation).transpose(0, 2, 1)[:, :, None, :]
    half = ROT_DIM // 2
    invf = (ROPE_THETA ** (-(jnp.arange(half, dtype=F32) * 2.0 / ROT_DIM))).reshape(half, 1)
    col0 = (d * 3) // GROUP_COLS
    n_grp = len(DILATION_GROUPS)
    wspec = lambda blk: pl.BlockSpec((d, GROUP_COLS), lambda i, r, n: (0, blk))
    tile = lambda width: pl.BlockSpec((1, n_sub, tq, width), lambda i, r, n: (i, r, n, 0))
    return pl.pallas_call(
        _att_kernel,
        grid=(b, dilation // n_sub, sub_len // tq),
        in_specs=[tile(d),
                  pl.BlockSpec((1, n_sub, 1, tq), lambda i, r, n: (i, r, 0, n)),
                  pl.BlockSpec((half, 1), lambda i, r, n: (0, 0)),
                  wspec(col0 + gi), wspec(col0 + n_grp + gi), wspec(col0 + 2 * n_grp + gi)],
        out_specs=[tile(GROUP_COLS), tile(LANES)],
        out_shape=[jax.ShapeDtypeStruct((b, dilation, sub_len, GROUP_COLS), BF16),
                   jax.ShapeDtypeStruct((b, dilation, sub_len, LANES), F32)],
        scratch_shapes=[pltpu.VMEM((HEADS_PER_VREG, n_sub * tq, GROUP_COLS), BF16),
                        pltpu.VMEM((QBLK + n_sub * tq, GROUP_COLS), BF16),
                        pltpu.VMEM((QBLK + n_sub * tq, GROUP_COLS), BF16)],
        compiler_params=pltpu.CompilerParams(
            dimension_semantics=("arbitrary", "arbitrary", "arbitrary"),
            vmem_limit_bytes=VMEM_LIMIT_BYTES),
        name=f"att_d{dilation}",
    )(hp, pos, invf, w_in, w_in, w_in)


def _final_kernel(x_ref, h_ref, gc_ref, *rest, dilations, final_norm):
    ng = len(dilations)
    o_refs, l_refs = rest[:ng], rest[ng:2 * ng]
    ex_ref, wz_ref, wg_ref, wao_ref, wo_ref, gate_ref, fg_ref, out_ref = rest[2 * ng:2 * ng + 8]
    slabs = rest[2 * ng + 8:]
    tm = x_ref.shape[1]

    for gi, dl in enumerate(dilations):
        if dl == 1:
            continue
        o_slab, l_slab = slabs[2 * (gi - 1)], slabs[2 * (gi - 1) + 1]
        for r in range(dl):
            rows = pl.ds(r, tm // dl, stride=dl)
            l_slab[rows, :] = l_refs[gi][0, r]
            for p in range(PAIRS):
                o_slab[p, rows, :] = o_refs[gi][0, r, :, p * LANES:(p + 1) * LANES].astype(F32)

    sub = tm // ROW_SPLIT
    head_lane = lax.broadcasted_iota(jnp.int32, (sub, LANES), 1) < HEADS_PER_GROUP

    def rows_of(sb):
        return slice(sb * sub, (sb + 1) * sub)

    def stage_gates(sb, st):
        h = h_ref[0, rows_of(sb), :]
        st["z"] = _dot(h, wz_ref[...])
        st["g"] = _dot(h, wg_ref[...])

    def stage_merge(sb, st):
        rows = rows_of(sb)
        lses, o_nat = [], []
        for gi, dl in enumerate(dilations):
            if dl == 1:
                lses.append(l_refs[gi][0, 0, rows, :])
                o_nat.append([o_refs[gi][0, 0, rows, p * LANES:(p + 1) * LANES].astype(F32)
                              for p in range(PAIRS)])
            else:
                o_slab, l_slab = slabs[2 * (gi - 1)], slabs[2 * (gi - 1) + 1]
                lses.append(l_slab[rows, :])
                o_nat.append([o_slab[p, rows, :] for p in range(PAIRS)])
        m = functools.reduce(jnp.maximum, lses)
        es = [jnp.exp(l - m) for l in lses]
        dens = [pltpu.roll(l, LANES - HEADS_PER_GROUP, 1) for l in lses]
        z = functools.reduce(lambda a, c: a + c, [e * dn for e, dn in zip(es, dens)])
        inv = 1.0 / jnp.where(head_lane, z, 1.0)
        wfs = []
        for e in es:
            w = jnp.where(head_lane, e * inv, 0.0)
            wfs.append(_dot(w.astype(BF16), ex_ref[...]))
        att = jnp.concatenate(
            [functools.reduce(lambda a, c: a + c,
                              [wf[:, p * LANES:(p + 1) * LANES] * o[p] for wf, o in zip(wfs, o_nat)])
             for p in range(PAIRS)], axis=1)
        st["a"] = (att * _silu(st.pop("z"))).astype(BF16)

    def stage_att_out(sb, st):
        st["ya"] = _dot(st.pop("a"), wao_ref[...])

    def stage_gate_merge(sb, st):
        merged = gc_ref[0, rows_of(sb), :].astype(F32) + _sigmoid(st.pop("g")) * st.pop("ya")
        st["merged"] = merged.astype(BF16)

    def stage_w_o(sb, st):
        st["r"] = _dot(st.pop("merged"), wo_ref[...])

    def stage_residual(sb, st):
        xn = x_ref[0, rows_of(sb), :] + gate_ref[0] * st.pop("r")
        if final_norm:
            xn = xn * lax.rsqrt(jnp.mean(xn * xn, axis=-1, keepdims=True) + EPS) * fg_ref[...]
        out_ref[0, rows_of(sb), :] = xn

    _emit_skewed((stage_gates, stage_merge, stage_att_out, stage_gate_merge, stage_w_o, stage_residual),
                 ROW_SPLIT)


def _merge_and_project(x, h, gc, outs, lses, w_in, w_att_out, w_o, gate, final_g, final_norm, tm):
    b, s, d = x.shape
    dilations = tuple(o.shape[1] for o in outs)
    assert dilations[0] == 1
    tok = lambda width: pl.BlockSpec((1, tm, width), lambda i, j: (i, j, 0))
    perm = lambda dl, width: pl.BlockSpec((1, dl, tm // dl, width), lambda i, j: (i, 0, j, 0))
    head = jnp.arange(LANES)[:, None]
    expand = (head == jnp.arange(GROUP_COLS)[None, :] // HEAD_DIM).astype(BF16)
    z_blk = w_in.shape[1] // GROUP_COLS - 2 * (d // GROUP_COLS) - 1
    g_blk = w_in.shape[1] // d - 1
    slabs = []
    for dl in dilations[1:]:
        slabs += [pltpu.VMEM((PAIRS, tm, LANES), F32), pltpu.VMEM((tm, LANES), F32)]
    return pl.pallas_call(
        functools.partial(_final_kernel, dilations=dilations, final_norm=final_norm),
        grid=(b, s // tm),
        in_specs=[tok(d), tok(d), tok(d)]
                 + [perm(dl, GROUP_COLS) for dl in dilations] + [perm(dl, LANES) for dl in dilations]
                 + [pl.BlockSpec((LANES, GROUP_COLS), lambda i, j: (0, 0)),
                    pl.BlockSpec((d, GROUP_COLS), lambda i, j: (0, z_blk)),
                    pl.BlockSpec((d, d), lambda i, j: (0, g_blk)),
                    pl.BlockSpec((GROUP_COLS, d), lambda i, j: (0, 0)),
                    pl.BlockSpec((d, d), lambda i, j: (0, 0)),
                    pl.BlockSpec((1, 1, d), lambda i, j: (i, 0, 0)),
                    pl.BlockSpec((1, d), lambda i, j: (0, 0))],
        out_specs=tok(d),
        out_shape=jax.ShapeDtypeStruct((b, s, d), F32),
        scratch_shapes=slabs,
        compiler_params=pltpu.CompilerParams(dimension_semantics=("arbitrary", "arbitrary"),
                                             vmem_limit_bytes=VMEM_LIMIT_BYTES),
        name="merge_out",
    )(x, h, gc, *outs, *lses, expand, w_in, w_in, w_att_out, w_o, gate.reshape(b, 1, d),
      final_g.reshape(1, d))


def kernel(x, c, positions, norm_g, w_ada, b_ada, w_in, conv_w, conv_b, conv_ln_g, conv_ln_b,
           w_conv_out, w_att_out, w_o, final_g):
    b, s, d = x.shape
    depth = norm_g.shape[0]
    pos_f = positions.astype(F32)
    dilations = tuple(dl for _, dl in DILATION_GROUPS)
    assert all(window == QBLK * dl for window, dl in DILATION_GROUPS)
    for layer in range(depth):
        mod = _modulation(c, w_ada[layer], b_ada[layer])
        shift, scale, gate = mod[:, :d], mod[:, d:2 * d], mod[:, 2 * d:]
        w_in_b = w_in[layer].astype(BF16)
        h, *h_perm = _modulated_norm(x, norm_g[layer], scale, shift, dilations, ts=1024)
        gc = _conv_branch(h, w_in_b, conv_w[layer], conv_b[layer], conv_ln_g[layer],
                          conv_ln_b[layer], w_conv_out[layer].astype(BF16), tm=1024)
        outs, lses = [], []
        for gi, hp in enumerate([h[:, None]] + h_perm):
            o, lse = _attention_group(hp, pos_f, w_in_b, gi, rows_per_step=1024)
            outs.append(o)
            lses.append(lse)
        x = _merge_and_project(x, h, gc, outs, lses, w_in_b, w_att_out[layer].astype(BF16),
                               w_o[layer].astype(BF16), gate, final_g,
                               final_norm=layer == depth - 1, tm=1024)
    return x
```

```python
import functools

import jax
import jax.numpy as jnp
from jax import lax
from jax.experimental import pallas as pl
from jax.experimental.pallas import tpu as pltpu

HEAD_DIM = 64
HEADS_PER_GROUP = 8
DILATION_GROUPS = ((128, 1), (512, 4), (2048, 16))
ROT_DIM = HEAD_DIM // 4
ROPE_THETA = 500000.0
CONV_SIZE = 31
EPS = 1e-6
NEG_INF = -1e30

LANES = 128
SUBLANES = 8
VMEM_LIMIT_BYTES = 56 * 1024 * 1024

GROUP_COLS = HEADS_PER_GROUP * HEAD_DIM
HEADS_PER_VREG = LANES // HEAD_DIM
PAIRS = GROUP_COLS // LANES
QBLK = 128
CONV_HALO = 32
CONV_ROWS = 64
NORM_ROWS = 128
ROW_SPLIT = 4
ATT_SPLIT = 2

F32 = jnp.float32
BF16 = jnp.bfloat16


def _dot(a, b):
    return jnp.dot(a, b, preferred_element_type=F32)


def _dot_nt(a, b):
    return lax.dot_general(a, b, (((1,), (1,)), ((), ())), preferred_element_type=F32)


def _sigmoid(x):
    return 1.0 / (1.0 + jnp.exp(-x))


def _silu(x):
    return x * _sigmoid(x)


def _emit_skewed(stages, n_blocks):
    state = [dict() for _ in range(n_blocks)]
    for tick in range(len(stages) + n_blocks - 1):
        for blk in range(n_blocks):
            if 0 <= tick - blk < len(stages):
                stages[tick - blk](blk, state[blk])


def _mod_kernel(c_ref, w_ref, b_ref, o_ref):
    o_ref[...] = jnp.dot(c_ref[...], w_ref[...], preferred_element_type=F32,
                         precision=lax.Precision.HIGHEST) + b_ref[...]


def _modulation(c, w_ada, b_ada):
    b, d = c.shape
    n = w_ada.shape[1]
    return pl.pallas_call(
        _mod_kernel,
        grid=(n // d,),
        in_specs=[pl.BlockSpec((b, d), lambda j: (0, 0)),
                  pl.BlockSpec((d, d), lambda j: (0, j)),
                  pl.BlockSpec((1, d), lambda j: (0, j))],
        out_specs=pl.BlockSpec((b, d), lambda j: (0, j)),
        out_shape=jax.ShapeDtypeStruct((b, n), F32),
        compiler_params=pltpu.CompilerParams(dimension_semantics=("arbitrary",),
                                             vmem_limit_bytes=VMEM_LIMIT_BYTES),
        name="mod",
    )(c, w_ada, b_ada.reshape(1, n))


def _h_kernel(x_ref, g_ref, scale_ref, shift_ref, h_ref, *rest, ratios):
    perm_refs, slabs = rest[:len(ratios)], rest[len(ratios):]
    ts, d = x_ref.shape[1], x_ref.shape[2]
    ncb = d // LANES
    gain = g_ref[...] * (1.0 + scale_ref[0])
    for rb in range(ts // NORM_ROWS):
        rows = slice(rb * NORM_ROWS, (rb + 1) * NORM_ROWS)
        x = x_ref[0, rows, :]
        y = x * lax.rsqrt(jnp.mean(x * x, axis=-1, keepdims=True) + EPS)
        hf = y * gain + shift_ref[0]
        h_ref[0, rows, :] = hf.astype(BF16)
        for cb in range(ncb):
            slabs[0][0, cb, rows, :] = hf[:, cb * LANES:(cb + 1) * LANES]
    n_prev, rows = 1, ts
    for lvl, ratio in enumerate(ratios):
        src, out_ref = slabs[lvl], perm_refs[lvl]
        dst = slabs[lvl + 1] if lvl + 1 < len(ratios) else None
        for rp in range(n_prev):
            for q in range(ratio):
                r = q * n_prev + rp
                for cb in range(ncb):
                    part = src[rp, cb, pl.ds(q, rows // ratio, stride=ratio), :]
                    if dst is not None:
                        dst[r, cb] = part
                    out_ref[0, r, :, cb * LANES:(cb + 1) * LANES] = part.astype(BF16)
        n_prev, rows = n_prev * ratio, rows // ratio


def _modulated_norm(x, g, scale, shift, dilations, ts):
    b, s, d = x.shape
    ratios = tuple(dn // dp for dp, dn in zip(dilations[:-1], dilations[1:]))
    assert dilations[0] == 1 and all(dp * r == dn for dp, dn, r in zip(dilations, dilations[1:], ratios))
    ncb = d // LANES
    vec = pl.BlockSpec((1, 1, d), lambda i, j: (i, 0, 0))
    perm_shapes = [jax.ShapeDtypeStruct((b, dl, s // dl, d), BF16) for dl in dilations[1:]]
    perm_specs = [pl.BlockSpec((1, dl, ts // dl, d), lambda i, j: (i, 0, j, 0)) for dl in dilations[1:]]
    slabs = [pltpu.VMEM((dl, ncb, ts // dl, LANES), F32) for dl in dilations[:-1]]
    return pl.pallas_call(
        functools.partial(_h_kernel, ratios=ratios),
        grid=(b, s // ts),
        in_specs=[pl.BlockSpec((1, ts, d), lambda i, j: (i, j, 0)),
                  pl.BlockSpec((1, d), lambda i, j: (0, 0)), vec, vec],
        out_specs=[pl.BlockSpec((1, ts, d), lambda i, j: (i, j, 0))] + perm_specs,
        out_shape=[jax.ShapeDtypeStruct((b, s, d), BF16)] + perm_shapes,
        scratch_shapes=slabs,
        compiler_params=pltpu.CompilerParams(dimension_semantics=("arbitrary", "arbitrary"),
                                             vmem_limit_bytes=VMEM_LIMIT_BYTES),
        name="hnorm",
    )(x, g.reshape(1, d), scale.reshape(b, 1, d), shift.reshape(b, 1, d))


def _conv_kernel(h_ref, wa_ref, wb_ref, wz_ref, wg_ref, cw_ref, cb_ref, lg_ref, lb_ref, wo_ref,
                 out_ref, ubuf, cbuf):
    tm = h_ref.shape[1]
    ncb = ubuf.shape[0]

    @pl.when(pl.program_id(1) == 0)
    def _():
        ubuf[:, 0:CONV_HALO, :] = jnp.zeros((ncb, CONV_HALO, LANES), F32)

    @pl.when(pl.program_id(1) > 0)
    def _():
        ubuf[:, 0:CONV_HALO, :] = ubuf[:, tm:tm + CONV_HALO, :]

    sub = tm // ROW_SPLIT

    def rows_of(sb):
        return slice(sb * sub, (sb + 1) * sub)

    def stage_glu_dots(sb, st):
        hs = h_ref[0, rows_of(sb), :]
        st["a"] = _dot(hs, wa_ref[...])
        st["b"] = _dot(hs, wb_ref[...])

    def stage_glu(sb, st):
        u = st.pop("a") * _sigmoid(st.pop("b"))
        for cb in range(ncb):
            ubuf[cb, CONV_HALO + sb * sub:CONV_HALO + (sb + 1) * sub, :] = u[:, cb * LANES:(cb + 1) * LANES]

    _emit_skewed((stage_glu_dots, stage_glu), ROW_SPLIT)

    first_tap = CONV_HALO - (CONV_SIZE - 1)

    def row_block(rb, carry):
        r0 = pl.multiple_of(rb * CONV_ROWS, CONV_ROWS)
        for cb in range(ncb):
            cols = slice(cb * LANES, (cb + 1) * LANES)
            acc = jnp.broadcast_to(cb_ref[:, cols], (CONV_ROWS, LANES))
            for t in range(CONV_SIZE):
                acc = acc + ubuf[cb, pl.ds(r0 + first_tap + t, CONV_ROWS), :] * cw_ref[t:t + 1, cols]
            cbuf[pl.ds(r0, CONV_ROWS), cols] = acc
        return carry

    lax.fori_loop(0, tm // CONV_ROWS, row_block, 0)

    def stage_z(sb, st):
        st["z"] = _dot(h_ref[0, rows_of(sb), :], wz_ref[...])

    def stage_norm(sb, st):
        c = cbuf[rows_of(sb), :]
        mu = jnp.mean(c, axis=-1, keepdims=True)
        dlt = c - mu
        var = jnp.mean(dlt * dlt, axis=-1, keepdims=True)
        st["y"] = _silu(dlt * lax.rsqrt(var + EPS) * lg_ref[...] + lb_ref[...])

    def stage_gate(sb, st):
        st["y"] = (st.pop("y") * _silu(st.pop("z"))).astype(BF16)

    def stage_out(sb, st):
        st["yc"] = _dot(st.pop("y"), wo_ref[...])

    def stage_g(sb, st):
        st["g"] = _dot(h_ref[0, rows_of(sb), :], wg_ref[...])

    def stage_store(sb, st):
        out_ref[0, rows_of(sb), :] = (_sigmoid(st.pop("g")) * st.pop("yc")).astype(BF16)

    _emit_skewed((stage_z, stage_norm, stage_gate, stage_out, stage_g, stage_store), ROW_SPLIT)


def _conv_branch(h, w_in, conv_w, conv_b, ln_g, ln_b, w_out, tm):
    b, s, d = h.shape
    cw = conv_w.shape[1]
    wcol = lambda blk: pl.BlockSpec((d, cw), lambda i, j: (0, blk))
    row = pl.BlockSpec((1, cw), lambda i, j: (0, 0))
    return pl.pallas_call(
        _conv_kernel,
        grid=(b, s // tm),
        in_specs=[pl.BlockSpec((1, tm, d), lambda i, j: (i, j, 0)),
                  wcol(0), wcol(1), wcol(2), wcol(8),
                  pl.BlockSpec((CONV_SIZE, cw), lambda i, j: (0, 0)), row, row, row,
                  pl.BlockSpec((cw, d), lambda i, j: (0, 0))],
        out_specs=pl.BlockSpec((1, tm, d), lambda i, j: (i, j, 0)),
        out_shape=jax.ShapeDtypeStruct((b, s, d), BF16),
        scratch_shapes=[pltpu.VMEM((cw // LANES, CONV_HALO + tm, LANES), F32),
                        pltpu.VMEM((tm, cw), F32)],
        compiler_params=pltpu.CompilerParams(dimension_semantics=("arbitrary", "arbitrary"),
                                             vmem_limit_bytes=VMEM_LIMIT_BYTES),
        name="conv_branch",
    )(h, w_in, w_in, w_in, w_in, conv_w, conv_b.reshape(1, cw), ln_g.reshape(1, cw),
      ln_b.reshape(1, cw), w_out)


def _att_kernel(h_ref, pos_ref, invf_ref, wq_ref, wk_ref, wv_ref, o_ref, lse_ref,
                qbuf, kbuf, vbuf):
    n_sub, tq, d = h_ref.shape[1], h_ref.shape[2], h_ref.shape[3]
    rows_all = n_sub * tq
    part = rows_all // ATT_SPLIT
    n = pl.program_id(2)

    @pl.when(n == 0)
    def _():
        kbuf[0:QBLK, :] = jnp.zeros((QBLK, GROUP_COLS), BF16)
        vbuf[0:QBLK, :] = jnp.zeros((QBLK, GROUP_COLS), BF16)

    @pl.when(n > 0)
    def _():
        kbuf[0:QBLK, :] = kbuf[rows_all:rows_all + QBLK, :]
        vbuf[0:QBLK, :] = vbuf[rows_all:rows_all + QBLK, :]

    lane_t = lax.broadcasted_iota(jnp.int32, (part, LANES), 1)
    low_half = (lane_t & (HEAD_DIM - 1)) < (ROT_DIM // 2)
    head0_t = lane_t < HEAD_DIM
    row = lax.broadcasted_iota(jnp.int32, (QBLK, 2 * QBLK), 0)
    col = lax.broadcasted_iota(jnp.int32, (QBLK, 2 * QBLK), 1)
    band = ((col < QBLK) & (col >= row)) | ((col >= QBLK) & (col - QBLK <= row))
    band_new_sub = band & (col >= QBLK)
    band_first = band & ((col >= QBLK) | (n > 0))
    lane = lax.broadcasted_iota(jnp.int32, (QBLK, LANES), 1)
    blocks_per_sub = tq // QBLK

    def stage_project(blk, st):
        hs = h_ref[0].reshape(rows_all, d)[blk * part:(blk + 1) * part, :]
        st["q"] = _dot(hs, wq_ref[...])
        st["k"] = _dot(hs, wk_ref[...])
        vbuf[QBLK + blk * part:QBLK + (blk + 1) * part, :] = _dot(hs, wv_ref[...]).astype(BF16)

    def stage_rotary(blk, st):
        pos = jnp.concatenate([pos_ref[0, r] for r in range(n_sub)], axis=1)[:, blk * part:(blk + 1) * part]
        ang = invf_ref[...] * pos
        cos_a, sin_a = jnp.cos(ang), jnp.sin(ang)
        rest = (HEAD_DIM - ROT_DIM, part)
        cs = jnp.concatenate([cos_a, cos_a, jnp.ones(rest, F32)] * HEADS_PER_VREG, axis=0).T
        sn = jnp.concatenate([-sin_a, sin_a, jnp.zeros(rest, F32)] * HEADS_PER_VREG, axis=0).T

        def rope(t):
            partner = jnp.where(low_half, pltpu.roll(t, LANES - ROT_DIM // 2, 1),
                                pltpu.roll(t, ROT_DIM // 2, 1))
            return t * cs + partner * sn

        q, k = st.pop("q"), st.pop("k")
        rows = slice(blk * part, (blk + 1) * part)
        for p in range(PAIRS):
            cols = slice(p * LANES, (p + 1) * LANES)
            qr = rope(q[:, cols]) * (HEAD_DIM ** -0.5)
            qbuf[0, rows, cols] = jnp.where(head0_t, qr, 0.0).astype(BF16)
            qbuf[1, rows, cols] = jnp.where(head0_t, 0.0, qr).astype(BF16)
            kbuf[QBLK + blk * part:QBLK + (blk + 1) * part, cols] = rope(k[:, cols]).astype(BF16)

    def stage_attend(blk, st):
        for qb in range(blk * part // QBLK, (blk + 1) * part // QBLK):
            r, qs = qb // blocks_per_sub, qb % blocks_per_sub
            rows = slice(qb * QBLK, (qb + 1) * QBLK)
            win = slice(qb * QBLK, (qb + 2) * QBLK)
            sub_rows = slice(qs * QBLK, (qs + 1) * QBLK)
            valid = band if qs else (band_new_sub if r else band_first)
            stats = jnp.zeros((QBLK, LANES), F32)
            for p in range(PAIRS):
                cols = slice(p * LANES, (p + 1) * LANES)
                kw = kbuf[win, cols]
                vw = vbuf[win, cols]
                outs = []
                for hh in range(HEADS_PER_VREG):
                    s = jnp.where(valid, _dot_nt(qbuf[hh, rows, cols], kw), NEG_INF)
                    m = jnp.max(s, axis=-1, keepdims=True)
                    e = jnp.exp(s - m)
                    den = jnp.sum(e, axis=-1, keepdims=True)
                    outs.append(_dot(e.astype(BF16), vw))
                    head = p * HEADS_PER_VREG + hh
                    stats = jnp.where(lane == head, m, jnp.where(lane == HEADS_PER_GROUP + head, den, stats))
                o_ref[0, r, sub_rows, cols] = jnp.where(lane < HEAD_DIM, outs[0], outs[1]).astype(BF16)
            lse_ref[0, r, sub_rows, :] = stats

    _emit_skewed((stage_project, stage_rotary, stage_attend), ATT_SPLIT)


def _attention_group(hp, pos_f, w_in, gi, rows_per_step):
    b, dilation, sub_len, d = hp.shape
    tq = min(rows_per_step, sub_len)
    n_sub = min(rows_per_step // tq, dilation)
    assert n_sub == 1 or tq == sub_len
    pos = pos_f.reshape(b, sub_len, dilation).transpose(0, 2, 1)[:, :, None, :]
    half = ROT_DIM // 2
    invf = (ROPE_THETA ** (-(jnp.arange(half, dtype=F32) * 2.0 / ROT_DIM))).reshape(half, 1)
    col0 = (d * 3) // GROUP_COLS
    n_grp = len(DILATION_GROUPS)
    wspec = lambda blk: pl.BlockSpec((d, GROUP_COLS), lambda i, r, n: (0, blk))
    tile = lambda width: pl.BlockSpec((1, n_sub, tq, width), lambda i, r, n: (i, r, n, 0))
    return pl.pallas_call(
        _att_kernel,
        grid=(b, dilation // n_sub, sub_len // tq),
        in_specs=[tile(d),
                  pl.BlockSpec((1, n_sub, 1, tq), lambda i, r, n: (i, r, 0, n)),
                  pl.BlockSpec((half, 1), lambda i, r, n: (0, 0)),
                  wspec(col0 + gi), wspec(col0 + n_grp + gi), wspec(col0 + 2 * n_grp + gi)],
        out_specs=[tile(GROUP_COLS), tile(LANES)],
        out_shape=[jax.ShapeDtypeStruct((b, dilation, sub_len, GROUP_COLS), BF16),
                   jax.ShapeDtypeStruct((b, dilation, sub_len, LANES), F32)],
        scratch_shapes=[pltpu.VMEM((HEADS_PER_VREG, n_sub * tq, GROUP_COLS), BF16),
                        pltpu.VMEM((QBLK + n_sub * tq, GROUP_COLS), BF16),
                        pltpu.VMEM((QBLK + n_sub * tq, GROUP_COLS), BF16)],
        compiler_params=pltpu.CompilerParams(
            dimension_semantics=("arbitrary", "arbitrary", "arbitrary"),
            vmem_limit_bytes=VMEM_LIMIT_BYTES),
        name=f"att_d{dilation}",
    )(hp, pos, invf, w_in, w_in, w_in)


def _final_kernel(x_ref, h_ref, gc_ref, *rest, dilations, final_norm):
    ng = len(dilations)
    o_refs, l_refs = rest[:ng], rest[ng:2 * ng]
    ex_ref, wz_ref, wg_ref, wao_ref, wo_ref, gate_ref, fg_ref, out_ref = rest[2 * ng:2 * ng + 8]
    slabs = rest[2 * ng + 8:]
    tm = x_ref.shape[1]

    for gi, dl in enumerate(dilations):
        if dl == 1:
            continue
        o_slab, l_slab = slabs[2 * (gi - 1)], slabs[2 * (gi - 1) + 1]
        for r in range(dl):
            rows = pl.ds(r, tm // dl, stride=dl)
            l_slab[rows, :] = l_refs[gi][0, r]
            for p in range(PAIRS):
                o_slab[p, rows, :] = o_refs[gi][0, r, :, p * LANES:(p + 1) * LANES].astype(F32)

    sub = tm // ROW_SPLIT
    head_lane = lax.broadcasted_iota(jnp.int32, (sub, LANES), 1) < HEADS_PER_GROUP

    def rows_of(sb):
        return slice(sb * sub, (sb + 1) * sub)

    def stage_gates(sb, st):
        h = h_ref[0, rows_of(sb), :]
        st["z"] = _dot(h, wz_ref[...])
        st["g"] = _dot(h, wg_ref[...])

    def stage_merge(sb, st):
        rows = rows_of(sb)
        lses, o_nat = [], []
        for gi, dl in enumerate(dilations):
            if dl == 1:
                lses.append(l_refs[gi][0, 0, rows, :])
                o_nat.append([o_refs[gi][0, 0, rows, p * LANES:(p + 1) * LANES].astype(F32)
                              for p in range(PAIRS)])
            else:
                o_slab, l_slab = slabs[2 * (gi - 1)], slabs[2 * (gi - 1) + 1]
                lses.append(l_slab[rows, :])
                o_nat.append([o_slab[p, rows, :] for p in range(PAIRS)])
        m = functools.reduce(jnp.maximum, lses)
        es = [jnp.exp(l - m) for l in lses]
        dens = [pltpu.roll(l, LANES - HEADS_PER_GROUP, 1) for l in lses]
        z = functools.reduce(lambda a, c: a + c, [e * dn for e, dn in zip(es, dens)])
        inv = 1.0 / jnp.where(head_lane, z, 1.0)
        wfs = []
        for e in es:
            w = jnp.where(head_lane, e * inv, 0.0)
            wfs.append(_dot(w.astype(BF16), ex_ref[...]))
        att = jnp.concatenate(
            [functools.reduce(lambda a, c: a + c,
                              [wf[:, p * LANES:(p + 1) * LANES] * o[p] for wf, o in zip(wfs, o_nat)])
             for p in range(PAIRS)], axis=1)
        st["a"] = (att * _silu(st.pop("z"))).astype(BF16)

    def stage_att_out(sb, st):
        st["ya"] = _dot(st.pop("a"), wao_ref[...])

    def stage_gate_merge(sb, st):
        merged = gc_ref[0, rows_of(sb), :].astype(F32) + _sigmoid(st.pop("g")) * st.pop("ya")
        st["merged"] = merged.astype(BF16)

    def stage_w_o(sb, st):
        st["r"] = _dot(st.pop("merged"), wo_ref[...])

    def stage_residual(sb, st):
        xn = x_ref[0, rows_of(sb), :] + gate_ref[0] * st.pop("r")
        if final_norm:
            xn = xn * lax.rsqrt(jnp.mean(xn * xn, axis=-1, keepdims=True) + EPS) * fg_ref[...]
        out_ref[0, rows_of(sb), :] = xn

    _emit_skewed((stage_gates, stage_merge, stage_att_out, stage_gate_merge, stage_w_o, stage_residual),
                 ROW_SPLIT)


def _merge_and_project(x, h, gc, outs, lses, w_in, w_att_out, w_o, gate, final_g, final_norm, tm):
    b, s, d = x.shape
    dilations = tuple(o.shape[1] for o in outs)
    assert dilations[0] == 1
    tok = lambda width: pl.BlockSpec((1, tm, width), lambda i, j: (i, j, 0))
    perm = lambda dl, width: pl.BlockSpec((1, dl, tm // dl, width), lambda i, j: (i, 0, j, 0))
    head = jnp.arange(LANES)[:, None]
    expand = (head == jnp.arange(GROUP_COLS)[None, :] // HEAD_DIM).astype(BF16)
    z_blk = w_in.shape[1] // GROUP_COLS - 2 * (d // GROUP_COLS) - 1
    g_blk = w_in.shape[1] // d - 1
    slabs = []
    for dl in dilations[1:]:
        slabs += [pltpu.VMEM((PAIRS, tm, LANES), F32), pltpu.VMEM((tm, LANES), F32)]
    return pl.pallas_call(
        functools.partial(_final_kernel, dilations=dilations, final_norm=final_norm),
        grid=(b, s // tm),
        in_specs=[tok(d), tok(d), tok(d)]
                 + [perm(dl, GROUP_COLS) for dl in dilations] + [perm(dl, LANES) for dl in dilations]
                 + [pl.BlockSpec((LANES, GROUP_COLS), lambda i, j: (0, 0)),
                    pl.BlockSpec((d, GROUP_COLS), lambda i, j: (0, z_blk)),
                    pl.BlockSpec((d, d), lambda i, j: (0, g_blk)),
                    pl.BlockSpec((GROUP_COLS, d), lambda i, j: (0, 0)),
                    pl.BlockSpec((d, d), lambda i, j: (0, 0)),
                    pl.BlockSpec((1, 1, d), lambda i, j: (i, 0, 0)),
                    pl.BlockSpec((1, d), lambda i, j: (0, 0))],
        out_specs=tok(d),
        out_shape=jax.ShapeDtypeStruct((b, s, d), F32),
        scratch_shapes=slabs,
        compiler_params=pltpu.CompilerParams(dimension_semantics=("arbitrary", "arbitrary"),
                                             vmem_limit_bytes=VMEM_LIMIT_BYTES),
        name="merge_out",
    )(x, h, gc, *outs, *lses, expand, w_in, w_in, w_att_out, w_o, gate.reshape(b, 1, d),
      final_g.reshape(1, d))


def kernel(x, c, positions, norm_g, w_ada, b_ada, w_in, conv_w, conv_b, conv_ln_g, conv_ln_b,
           w_conv_out, w_att_out, w_o, final_g):
    b, s, d = x.shape
    depth = norm_g.shape[0]
    pos_f = positions.astype(F32)
    dilations = tuple(dl for _, dl in DILATION_GROUPS)
    assert all(window == QBLK * dl for window, dl in DILATION_GROUPS)
    for layer in range(depth):
        mod = _modulation(c, w_ada[layer], b_ada[layer])
        shift, scale, gate = mod[:, :d], mod[:, d:2 * d], mod[:, 2 * d:]
        w_in_b = w_in[layer].astype(BF16)
        h, *h_perm = _modulated_norm(x, norm_g[layer], scale, shift, dilations, ts=1024)
        gc = _conv_branch(h, w_in_b, conv_w[layer], conv_b[layer], conv_ln_g[layer],
                          conv_ln_b[layer], w_conv_out[layer].astype(BF16), tm=1024)
        outs, lses = [], []
        for gi, hp in enumerate([h[:, None]] + h_perm):
            o, lse = _attention_group(hp, pos_f, w_in_b, gi, rows_per_step=2048)
            outs.append(o)
            lses.append(lse)
        x = _merge_and_project(x, h, gc, outs, lses, w_in_b, w_att_out[layer].astype(BF16),
                               w_o[layer].astype(BF16), gate, final_g,
                               final_norm=layer == depth - 1, tm=1024)
    return x
```

```python
import functools

import jax
import jax.numpy as jnp
from jax import lax
from jax.experimental import pallas as pl
from jax.experimental.pallas import tpu as pltpu

HEAD_DIM = 64
HEADS_PER_GROUP = 8
DILATION_GROUPS = ((128, 1), (512, 4), (2048, 16))
ROT_DIM = HEAD_DIM // 4
ROPE_THETA = 500000.0
CONV_SIZE = 31
EPS = 1e-6
NEG_INF = -1e30

LANES = 128
SUBLANES = 8
VMEM_LIMIT_BYTES = 56 * 1024 * 1024

GROUP_COLS = HEADS_PER_GROUP * HEAD_DIM
HEADS_PER_VREG = LANES // HEAD_DIM
PAIRS = GROUP_COLS // LANES
QBLK = 128
CONV_HALO = 32
CONV_ROWS = 128
NORM_ROWS = 128
ROW_SPLIT = 4
ATT_SPLIT = 2

F32 = jnp.float32
BF16 = jnp.bfloat16


def _dot(a, b):
    return jnp.dot(a, b, preferred_element_type=F32)


def _dot_nt(a, b):
    return lax.dot_general(a, b, (((1,), (1,)), ((), ())), preferred_element_type=F32)


def _sigmoid(x):
    return 1.0 / (1.0 + jnp.exp(-x))


def _silu(x):
    return x * _sigmoid(x)


def _emit_skewed(stages, n_blocks):
    state = [dict() for _ in range(n_blocks)]
    for tick in range(len(stages) + n_blocks - 1):
        for blk in range(n_blocks):
            if 0 <= tick - blk < len(stages):
                stages[tick - blk](blk, state[blk])


def _mod_kernel(c_ref, w_ref, b_ref, o_ref):
    o_ref[...] = jnp.dot(c_ref[...], w_ref[...], preferred_element_type=F32,
                         precision=lax.Precision.HIGHEST) + b_ref[...]


def _modulation(c, w_ada, b_ada):
    b, d = c.shape
    n = w_ada.shape[1]
    return pl.pallas_call(
        _mod_kernel,
        grid=(n // d,),
        in_specs=[pl.BlockSpec((b, d), lambda j: (0, 0)),
                  pl.BlockSpec((d, d), lambda j: (0, j)),
                  pl.BlockSpec((1, d), lambda j: (0, j))],
        out_specs=pl.BlockSpec((b, d), lambda j: (0, j)),
        out_shape=jax.ShapeDtypeStruct((b, n), F32),
        compiler_params=pltpu.CompilerParams(dimension_semantics=("arbitrary",),
                                             vmem_limit_bytes=VMEM_LIMIT_BYTES),
        name="mod",
    )(c, w_ada, b_ada.reshape(1, n))


def _h_kernel(x_ref, g_ref, scale_ref, shift_ref, h_ref, *rest, ratios):
    perm_refs, slabs = rest[:len(ratios)], rest[len(ratios):]
    ts, d = x_ref.shape[1], x_ref.shape[2]
    ncb = d // LANES
    gain = g_ref[...] * (1.0 + scale_ref[0])
    for rb in range(ts // NORM_ROWS):
        rows = slice(rb * NORM_ROWS, (rb + 1) * NORM_ROWS)
        x = x_ref[0, rows, :]
        y = x * lax.rsqrt(jnp.mean(x * x, axis=-1, keepdims=True) + EPS)
        hf = y * gain + shift_ref[0]
        h_ref[0, rows, :] = hf.astype(BF16)
        for cb in range(ncb):
            slabs[0][0, cb, rows, :] = hf[:, cb * LANES:(cb + 1) * LANES]
    n_prev, rows = 1, ts
    for lvl, ratio in enumerate(ratios):
        src, out_ref = slabs[lvl], perm_refs[lvl]
        dst = slabs[lvl + 1] if lvl + 1 < len(ratios) else None
        for rp in range(n_prev):
            for q in range(ratio):
                r = q * n_prev + rp
                for cb in range(ncb):
                    part = src[rp, cb, pl.ds(q, rows // ratio, stride=ratio), :]
                    if dst is not None:
                        dst[r, cb] = part
                    out_ref[0, r, :, cb * LANES:(cb + 1) * LANES] = part.astype(BF16)
        n_prev, rows = n_prev * ratio, rows // ratio


def _modulated_norm(x, g, scale, shift, dilations, ts):
    b, s, d = x.shape
    ratios = tuple(dn // dp for dp, dn in zip(dilations[:-1], dilations[1:]))
    assert dilations[0] == 1 and all(dp * r == dn for dp, dn, r in zip(dilations, dilations[1:], ratios))
    ncb = d // LANES
    vec = pl.BlockSpec((1, 1, d), lambda i, j: (i, 0, 0))
    perm_shapes = [jax.ShapeDtypeStruct((b, dl, s // dl, d), BF16) for dl in dilations[1:]]
    perm_specs = [pl.BlockSpec((1, dl, ts // dl, d), lambda i, j: (i, 0, j, 0)) for dl in dilations[1:]]
    slabs = [pltpu.VMEM((dl, ncb, ts // dl, LANES), F32) for dl in dilations[:-1]]
    return pl.pallas_call(
        functools.partial(_h_kernel, ratios=ratios),
        grid=(b, s // ts),
        in_specs=[pl.BlockSpec((1, ts, d), lambda i, j: (i, j, 0)),
                  pl.BlockSpec((1, d), lambda i, j: (0, 0)), vec, vec],
        out_specs=[pl.BlockSpec((1, ts, d), lambda i, j: (i, j, 0))] + perm_specs,
        out_shape=[jax.ShapeDtypeStruct((b, s, d), BF16)] + perm_shapes,
        scratch_shapes=slabs,
        compiler_params=pltpu.CompilerParams(dimension_semantics=("arbitrary", "arbitrary"),
                                             vmem_limit_bytes=VMEM_LIMIT_BYTES),
        name="hnorm",
    )(x, g.reshape(1, d), scale.reshape(b, 1, d), shift.reshape(b, 1, d))


def _conv_kernel(h_ref, wa_ref, wb_ref, wz_ref, wg_ref, cw_ref, cb_ref, lg_ref, lb_ref, wo_ref,
                 out_ref, ubuf, cbuf):
    tm = h_ref.shape[1]
    ncb = ubuf.shape[0]

    @pl.when(pl.program_id(1) == 0)
    def _():
        ubuf[:, 0:CONV_HALO, :] = jnp.zeros((ncb, CONV_HALO, LANES), F32)

    @pl.when(pl.program_id(1) > 0)
    def _():
        ubuf[:, 0:CONV_HALO, :] = ubuf[:, tm:tm + CONV_HALO, :]

    sub = tm // ROW_SPLIT

    def rows_of(sb):
        return slice(sb * sub, (sb + 1) * sub)

    def stage_glu_dots(sb, st):
        hs = h_ref[0, rows_of(sb), :]
        st["a"] = _dot(hs, wa_ref[...])
        st["b"] = _dot(hs, wb_ref[...])

    def stage_glu(sb, st):
        u = st.pop("a") * _sigmoid(st.pop("b"))
        for cb in range(ncb):
            ubuf[cb, CONV_HALO + sb * sub:CONV_HALO + (sb + 1) * sub, :] = u[:, cb * LANES:(cb + 1) * LANES]

    _emit_skewed((stage_glu_dots, stage_glu), ROW_SPLIT)

    first_tap = CONV_HALO - (CONV_SIZE - 1)

    def row_block(rb, carry):
        r0 = pl.multiple_of(rb * CONV_ROWS, CONV_ROWS)
        for cb in range(ncb):
            cols = slice(cb * LANES, (cb + 1) * LANES)
            acc = jnp.broadcast_to(cb_ref[:, cols], (CONV_ROWS, LANES))
            for t in range(CONV_SIZE):
                acc = acc + ubuf[cb, pl.ds(r0 + first_tap + t, CONV_ROWS), :] * cw_ref[t:t + 1, cols]
            cbuf[pl.ds(r0, CONV_ROWS), cols] = acc
        return carry

    lax.fori_loop(0, tm // CONV_ROWS, row_block, 0)

    def stage_z(sb, st):
        st["z"] = _dot(h_ref[0, rows_of(sb), :], wz_ref[...])

    def stage_norm(sb, st):
        c = cbuf[rows_of(sb), :]
        mu = jnp.mean(c, axis=-1, keepdims=True)
        dlt = c - mu
        var = jnp.mean(dlt * dlt, axis=-1, keepdims=True)
        st["y"] = _silu(dlt * lax.rsqrt(var + EPS) * lg_ref[...] + lb_ref[...])

    def stage_gate(sb, st):
        st["y"] = (st.pop("y") * _silu(st.pop("z"))).astype(BF16)

    def stage_out(sb, st):
        st["yc"] = _dot(st.pop("y"), wo_ref[...])

    def stage_g(sb, st):
        st["g"] = _dot(h_ref[0, rows_of(sb), :], wg_ref[...])

    def stage_store(sb, st):
        out_ref[0, rows_of(sb), :] = (_sigmoid(st.pop("g")) * st.pop("yc")).astype(BF16)

    _emit_skewed((stage_z, stage_norm, stage_gate, stage_out, stage_g, stage_store), ROW_SPLIT)


def _conv_branch(h, w_in, conv_w, conv_b, ln_g, ln_b, w_out, tm):
    b, s, d = h.shape
    cw = conv_w.shape[1]
    wcol = lambda blk: pl.BlockSpec((d, cw), lambda i, j: (0, blk))
    row = pl.BlockSpec((1, cw), lambda i, j: (0, 0))
    return pl.pallas_call(
        _conv_kernel,
        grid=(b, s // tm),
        in_specs=[pl.BlockSpec((1, tm, d), lambda i, j: (i, j, 0)),
                  wcol(0), wcol(1), wcol(2), wcol(8),
                  pl.BlockSpec((CONV_SIZE, cw), lambda i, j: (0, 0)), row, row, row,
                  pl.BlockSpec((cw, d), lambda i, j: (0, 0))],
        out_specs=pl.BlockSpec((1, tm, d), lambda i, j: (i, j, 0)),
        out_shape=jax.ShapeDtypeStruct((b, s, d), BF16),
        scratch_shapes=[pltpu.VMEM((cw // LANES, CONV_HALO + tm, LANES), F32),
                        pltpu.VMEM((tm, cw), F32)],
        compiler_params=pltpu.CompilerParams(dimension_semantics=("arbitrary", "arbitrary"),
                                             vmem_limit_bytes=VMEM_LIMIT_BYTES),
        name="conv_branch",
    )(h, w_in, w_in, w_in, w_in, conv_w, conv_b.reshape(1, cw), ln_g.reshape(1, cw),
      ln_b.reshape(1, cw), w_out)


def _att_kernel(h_ref, pos_ref, invf_ref, wq_ref, wk_ref, wv_ref, o_ref, lse_ref,
                qbuf, kbuf, vbuf):
    n_sub, tq, d = h_ref.shape[1], h_ref.shape[2], h_ref.shape[3]
    rows_all = n_sub * tq
    part = rows_all // ATT_SPLIT
    n = pl.program_id(2)

    @pl.when(n == 0)
    def _():
        kbuf[0:QBLK, :] = jnp.zeros((QBLK, GROUP_COLS), BF16)
        vbuf[0:QBLK, :] = jnp.zeros((QBLK, GROUP_COLS), BF16)

    @pl.when(n > 0)
    def _():
        kbuf[0:QBLK, :] = kbuf[rows_all:rows_all + QBLK, :]
        vbuf[0:QBLK, :] = vbuf[rows_all:rows_all + QBLK, :]

    lane_t = lax.broadcasted_iota(jnp.int32, (part, LANES), 1)
    low_half = (lane_t & (HEAD_DIM - 1)) < (ROT_DIM // 2)
    head0_t = lane_t < HEAD_DIM
    row = lax.broadcasted_iota(jnp.int32, (QBLK, 2 * QBLK), 0)
    col = lax.broadcasted_iota(jnp.int32, (QBLK, 2 * QBLK), 1)
    band = ((col < QBLK) & (col >= row)) | ((col >= QBLK) & (col - QBLK <= row))
    band_new_sub = band & (col >= QBLK)
    band_first = band & ((col >= QBLK) | (n > 0))
    lane = lax.broadcasted_iota(jnp.int32, (QBLK, LANES), 1)
    blocks_per_sub = tq // QBLK

    def stage_project(blk, st):
        hs = h_ref[0].reshape(rows_all, d)[blk * part:(blk + 1) * part, :]
        st["q"] = _dot(hs, wq_ref[...])
        st["k"] = _dot(hs, wk_ref[...])
        vbuf[QBLK + blk * part:QBLK + (blk + 1) * part, :] = _dot(hs, wv_ref[...]).astype(BF16)

    def stage_rotary(blk, st):
        pos = jnp.concatenate([pos_ref[0, r] for r in range(n_sub)], axis=1)[:, blk * part:(blk + 1) * part]
        ang = invf_ref[...] * pos
        cos_a, sin_a = jnp.cos(ang), jnp.sin(ang)
        rest = (HEAD_DIM - ROT_DIM, part)
        cs = jnp.concatenate([cos_a, cos_a, jnp.ones(rest, F32)] * HEADS_PER_VREG, axis=0).T
        sn = jnp.concatenate([-sin_a, sin_a, jnp.zeros(rest, F32)] * HEADS_PER_VREG, axis=0).T

        def rope(t):
            partner = jnp.where(low_half, pltpu.roll(t, LANES - ROT_DIM // 2, 1),
                                pltpu.roll(t, ROT_DIM // 2, 1))
            return t * cs + partner * sn

        q, k = st.pop("q"), st.pop("k")
        rows = slice(blk * part, (blk + 1) * part)
        for p in range(PAIRS):
            cols = slice(p * LANES, (p + 1) * LANES)
            qr = rope(q[:, cols]) * (HEAD_DIM ** -0.5)
            qbuf[0, rows, cols] = jnp.where(head0_t, qr, 0.0).astype(BF16)
            qbuf[1, rows, cols] = jnp.where(head0_t, 0.0, qr).astype(BF16)
            kbuf[QBLK + blk * part:QBLK + (blk + 1) * part, cols] = rope(k[:, cols]).astype(BF16)

    def stage_attend(blk, st):
        for qb in range(blk * part // QBLK, (blk + 1) * part // QBLK):
            r, qs = qb // blocks_per_sub, qb % blocks_per_sub
            rows = slice(qb * QBLK, (qb + 1) * QBLK)
            win = slice(qb * QBLK, (qb + 2) * QBLK)
            sub_rows = slice(qs * QBLK, (qs + 1) * QBLK)
            valid = band if qs else (band_new_sub if r else band_first)
            stats = jnp.zeros((QBLK, LANES), F32)
            for p in range(PAIRS):
                cols = slice(p * LANES, (p + 1) * LANES)
                kw = kbuf[win, cols]
                vw = vbuf[win, cols]
                qs2 = jnp.concatenate([qbuf[hh, rows, cols] for hh in range(HEADS_PER_VREG)], axis=0)
                valid2 = jnp.concatenate([valid] * HEADS_PER_VREG, axis=0)
                s = jnp.where(valid2, _dot_nt(qs2, kw), NEG_INF)
                m = jnp.max(s, axis=-1, keepdims=True)
                e = jnp.exp(s - m)
                den = jnp.sum(e, axis=-1, keepdims=True)
                pv = _dot(e.astype(BF16), vw)
                for hh in range(HEADS_PER_VREG):
                    hr = slice(hh * QBLK, (hh + 1) * QBLK)
                    head = p * HEADS_PER_VREG + hh
                    stats = jnp.where(lane == head, m[hr],
                                      jnp.where(lane == HEADS_PER_GROUP + head, den[hr], stats))
                o_ref[0, r, sub_rows, cols] = jnp.where(lane < HEAD_DIM, pv[0:QBLK], pv[QBLK:]).astype(BF16)
            lse_ref[0, r, sub_rows, :] = stats

    _emit_skewed((stage_project, stage_rotary, stage_attend), ATT_SPLIT)


def _attention_group(hp, pos_f, w_in, gi, rows_per_step):
    b, dilation, sub_len, d = hp.shape
    tq = min(rows_per_step, sub_len)
    n_sub = min(rows_per_step // tq, dilation)
    assert n_sub == 1 or tq == sub_len
    pos = pos_f.reshape(b, sub_len, dilation).transpose(0, 2, 1)[:, :, None, :]
    half = ROT_DIM // 2
    invf = (ROPE_THETA ** (-(jnp.arange(half, dtype=F32) * 2.0 / ROT_DIM))).reshape(half, 1)
    col0 = (d * 3) // GROUP_COLS
    n_grp = len(DILATION_GROUPS)
    wspec = lambda blk: pl.BlockSpec((d, GROUP_COLS), lambda i, r, n: (0, blk))
    tile = lambda width: pl.BlockSpec((1, n_sub, tq, width), lambda i, r, n: (i, r, n, 0))
    return pl.pallas_call(
        _att_kernel,
        grid=(b, dilation // n_sub, sub_len // tq),
        in_specs=[tile(d),
                  pl.BlockSpec((1, n_sub, 1, tq), lambda i, r, n: (i, r, 0, n)),
                  pl.BlockSpec((half, 1), lambda i, r, n: (0, 0)),
                  wspec(col0 + gi), wspec(col0 + n_grp + gi), wspec(col0 + 2 * n_grp + gi)],
        out_specs=[tile(GROUP_COLS), tile(LANES)],
        out_shape=[jax.ShapeDtypeStruct((b, dilation, sub_len, GROUP_COLS), BF16),
                   jax.ShapeDtypeStruct((b, dilation, sub_len, LANES), F32)],
        scratch_shapes=[pltpu.VMEM((HEADS_PER_VREG, n_sub * tq, GROUP_COLS), BF16),
                        pltpu.VMEM((QBLK + n_sub * tq, GROUP_COLS), BF16),
                        pltpu.VMEM((QBLK + n_sub * tq, GROUP_COLS), BF16)],
        compiler_params=pltpu.CompilerParams(
            dimension_semantics=("arbitrary", "arbitrary", "arbitrary"),
            vmem_limit_bytes=VMEM_LIMIT_BYTES),
        name=f"att_d{dilation}",
    )(hp, pos, invf, w_in, w_in, w_in)


def _final_kernel(x_ref, h_ref, gc_ref, *rest, dilations, final_norm):
    ng = len(dilations)
    o_refs, l_refs = rest[:ng], rest[ng:2 * ng]
    ex_ref, wz_ref, wg_ref, wao_ref, wo_ref, gate_ref, fg_ref, out_ref = rest[2 * ng:2 * ng + 8]
    slabs = rest[2 * ng + 8:]
    tm = x_ref.shape[1]

    for gi, dl in enumerate(dilations):
        if dl == 1:
            continue
        o_slab, l_slab = slabs[2 * (gi - 1)], slabs[2 * (gi - 1) + 1]
        for r in range(dl):
            rows = pl.ds(r, tm // dl, stride=dl)
            l_slab[rows, :] = l_refs[gi][0, r]
            for p in range(PAIRS):
                o_slab[p, rows, :] = o_refs[gi][0, r, :, p * LANES:(p + 1) * LANES].astype(F32)

    sub = tm // ROW_SPLIT
    head_lane = lax.broadcasted_iota(jnp.int32, (sub, LANES), 1) < HEADS_PER_GROUP

    def rows_of(sb):
        return slice(sb * sub, (sb + 1) * sub)

    def stage_gates(sb, st):
        h = h_ref[0, rows_of(sb), :]
        st["z"] = _dot(h, wz_ref[...])
        st["g"] = _dot(h, wg_ref[...])

    def stage_merge(sb, st):
        rows = rows_of(sb)
        lses, o_nat = [], []
        for gi, dl in enumerate(dilations):
            if dl == 1:
                lses.append(l_refs[gi][0, 0, rows, :])
                o_nat.append([o_refs[gi][0, 0, rows, p * LANES:(p + 1) * LANES].astype(F32)
                              for p in range(PAIRS)])
            else:
                o_slab, l_slab = slabs[2 * (gi - 1)], slabs[2 * (gi - 1) + 1]
                lses.append(l_slab[rows, :])
                o_nat.append([o_slab[p, rows, :] for p in range(PAIRS)])
        m = functools.reduce(jnp.maximum, lses)
        es = [jnp.exp(l - m) for l in lses]
        dens = [pltpu.roll(l, LANES - HEADS_PER_GROUP, 1) for l in lses]
        z = functools.reduce(lambda a, c: a + c, [e * dn for e, dn in zip(es, dens)])
        inv = 1.0 / jnp.where(head_lane, z, 1.0)
        wfs = []
        for e in es:
            w = jnp.where(head_lane, e * inv, 0.0)
            wfs.append(_dot(w.astype(BF16), ex_ref[...]))
        att = jnp.concatenate(
            [functools.reduce(lambda a, c: a + c,
                              [wf[:, p * LANES:(p + 1) * LANES] * o[p] for wf, o in zip(wfs, o_nat)])
             for p in range(PAIRS)], axis=1)
        st["a"] = (att * _silu(st.pop("z"))).astype(BF16)

    def stage_att_out(sb, st):
        st["ya"] = _dot(st.pop("a"), wao_ref[...])

    def stage_gate_merge(sb, st):
        merged = gc_ref[0, rows_of(sb), :].astype(F32) + _sigmoid(st.pop("g")) * st.pop("ya")
        st["merged"] = merged.astype(BF16)

    def stage_w_o(sb, st):
        st["r"] = _dot(st.pop("merged"), wo_ref[...])

    def stage_residual(sb, st):
        xn = x_ref[0, rows_of(sb), :] + gate_ref[0] * st.pop("r")
        if final_norm:
            xn = xn * lax.rsqrt(jnp.mean(xn * xn, axis=-1, keepdims=True) + EPS) * fg_ref[...]
        out_ref[0, rows_of(sb), :] = xn

    _emit_skewed((stage_gates, stage_merge, stage_att_out, stage_gate_merge, stage_w_o, stage_residual),
                 ROW_SPLIT)


def _merge_and_project(x, h, gc, outs, lses, w_in, w_att_out, w_o, gate, final_g, final_norm, tm):
    b, s, d = x.shape
    dilations = tuple(o.shape[1] for o in outs)
    assert dilations[0] == 1
    tok = lambda width: pl.BlockSpec((1, tm, width), lambda i, j: (i, j, 0))
    perm = lambda dl, width: pl.BlockSpec((1, dl, tm // dl, width), lambda i, j: (i, 0, j, 0))
    head = jnp.arange(LANES)[:, None]
    expand = (head == jnp.arange(GROUP_COLS)[None, :] // HEAD_DIM).astype(BF16)
    z_blk = w_in.shape[1] // GROUP_COLS - 2 * (d // GROUP_COLS) - 1
    g_blk = w_in.shape[1] // d - 1
    slabs = []
    for dl in dilations[1:]:
        slabs += [pltpu.VMEM((PAIRS, tm, LANES), F32), pltpu.VMEM((tm, LANES), F32)]
    return pl.pallas_call(
        functools.partial(_final_kernel, dilations=dilations, final_norm=final_norm),
        grid=(b, s // tm),
        in_specs=[tok(d), tok(d), tok(d)]
                 + [perm(dl, GROUP_COLS) for dl in dilations] + [perm(dl, LANES) for dl in dilations]
                 + [pl.BlockSpec((LANES, GROUP_COLS), lambda i, j: (0, 0)),
                    pl.BlockSpec((d, GROUP_COLS), lambda i, j: (0, z_blk)),
                    pl.BlockSpec((d, d), lambda i, j: (0, g_blk)),
                    pl.BlockSpec((GROUP_COLS, d), lambda i, j: (0, 0)),
                    pl.BlockSpec((d, d), lambda i, j: (0, 0)),
                    pl.BlockSpec((1, 1, d), lambda i, j: (i, 0, 0)),
                    pl.BlockSpec((1, d), lambda i, j: (0, 0))],
        out_specs=tok(d),
        out_shape=jax.ShapeDtypeStruct((b, s, d), F32),
        scratch_shapes=slabs,
        compiler_params=pltpu.CompilerParams(dimension_semantics=("arbitrary", "arbitrary"),
                                             vmem_limit_bytes=VMEM_LIMIT_BYTES),
        name="merge_out",
    )(x, h, gc, *outs, *lses, expand, w_in, w_in, w_att_out, w_o, gate.reshape(b, 1, d),
      final_g.reshape(1, d))


def kernel(x, c, positions, norm_g, w_ada, b_ada, w_in, conv_w, conv_b, conv_ln_g, conv_ln_b,
           w_conv_out, w_att_out, w_o, final_g):
    b, s, d = x.shape
    depth = norm_g.shape[0]
    pos_f = positions.astype(F32)
    dilations = tuple(dl for _, dl in DILATION_GROUPS)
    assert all(window == QBLK * dl for window, dl in DILATION_GROUPS)
    for layer in range(depth):
        mod = _modulation(c, w_ada[layer], b_ada[layer])
        shift, scale, gate = mod[:, :d], mod[:, d:2 * d], mod[:, 2 * d:]
        w_in_b = w_in[layer].astype(BF16)
        h, *h_perm = _modulated_norm(x, norm_g[layer], scale, shift, dilations, ts=1024)
        gc = _conv_branch(h, w_in_b, conv_w[layer], conv_b[layer], conv_ln_g[layer],
                          conv_ln_b[layer], w_conv_out[layer].astype(BF16), tm=1024)
        outs, lses = [], []
        for gi, hp in enumerate([h[:, None]] + h_perm):
            o, lse = _attention_group(hp, pos_f, w_in_b, gi, rows_per_step=2048)
            outs.append(o)
            lses.append(lse)
        x = _merge_and_project(x, h, gc, outs, lses, w_in_b, w_att_out[layer].astype(BF16),
                               w_o[layer].astype(BF16), gate, final_g,
                               final_norm=layer == depth - 1, tm=1024)
    return x
```
